```python
import math
import jax, jax.numpy as jnp
from jax import lax
import numpy as np

D_MODEL = 2048
BATCH = 16
SEQ = 256
DEPTH = 4
DEC_BATCH = 8
DEC_SEQ = 1024
PAST_LEN = 512

GRID_W = 64
N_EVEN = (DEPTH + 1) // 2
N_ODD = DEPTH // 2
N_DIR = 2
N_MOD = 6
NORM_EPS = 1e-6
CONV_LEFT = 2
DN_HEADS = 8
DN_DK = 128
DN_DV = 128
DN_CONV = 4
DN_CHUNK = 64
DN_QK = DN_HEADS * DN_DK
DN_VW = DN_HEADS * DN_DV
RW_HEADS = 16
RW_N = 64
RW_C = RW_HEADS * RW_N
RW_R_DECAY = 64
RW_R_A = 64
RW_R_GATE = 160
RW_GN_EPS = RW_N * 1e-5
LRU_W = D_MODEL
LRU_BLOCKS = 16
LRU_BS = LRU_W // LRU_BLOCKS
LRU_CONV = 4
LRU_C = 8.0
D_FF = 5632
FFN_K = 3
EV_A_SIZES = (2 * DN_QK + DN_VW, DN_VW, N_DIR * DN_HEADS, N_DIR * DN_HEADS)
EV_B_SIZES = (RW_C, RW_C, RW_C, N_DIR * RW_R_DECAY, N_DIR * RW_R_A, RW_R_GATE)
EV_A_IN = sum(EV_A_SIZES)
EV_B_IN = sum(EV_B_SIZES)
EV_IN = EV_A_IN + EV_B_IN
EV_OUT = DN_VW + RW_C
OD_IN = 2 * LRU_W

kernel_name = 'hybrid_delta_rwkv_rglru_flow_step'


def split_sizes(x, sizes):
    out, off = [], 0
    for s in sizes:
        out.append(x[..., off:off + s])
        off += s
    return out


def rms_norm(x, g):
    xf = x.astype(jnp.float32)
    y = xf * lax.rsqrt(jnp.mean(xf * xf, axis=-1, keepdims=True) + NORM_EPS)
    return (y * g.astype(jnp.float32)).astype(x.dtype)


def l2norm(x):
    xf = x.astype(jnp.float32)
    return xf * lax.rsqrt(jnp.sum(xf * xf, axis=-1, keepdims=True) + 1e-6)


def short_conv1d(x, w, left):
    k, length = w.shape[0], x.shape[1]
    xp = jnp.pad(x, ((0, 0), (left, k - 1 - left), (0, 0)))
    out = xp[:, 0:length] * w[0]
    for j in range(1, k):
        out = out + xp[:, j:j + length] * w[j]
    return out


def token_shift(x, mu_prev, mu_next):
    prev = jnp.pad(x, ((0, 0), (1, 0), (0, 0)))[:, :-1]
    nxt = jnp.pad(x, ((0, 0), (0, 1), (0, 0)))[:, 1:]
    return x + mu_prev * (prev - x) + mu_next * (nxt - x)


def dir_stack(x_f, x_b):
    y = jnp.stack([x_f, jnp.flip(x_b, axis=1)], axis=1)
    return y.reshape((-1,) + y.shape[2:])


def dir_merge(y, b):
    y = y.reshape((b, N_DIR) + y.shape[1:])
    return y[:, 0] + jnp.flip(y[:, 1], axis=1)


def gated_delta_chunked(q, k, v, beta, g, s0):
    z, length, h, dk = q.shape
    dv = v.shape[-1]
    cs = DN_CHUNK
    n = length // cs

    def chunks(t):
        t = t.reshape((z, n, cs, h) + t.shape[3:])
        return jnp.moveaxis(jnp.moveaxis(t, 1, 0), 3, 2)

    q, k, v, beta, g = chunks(q), chunks(k), chunks(v), chunks(beta), chunks(g)
    gc = jnp.cumsum(g, axis=-1)
    diff = gc[..., :, None] - gc[..., None, :]
    idx = jnp.arange(cs)
    incl = idx[:, None] >= idx[None, :]
    strict = idx[:, None] > idx[None, :]
    dec_incl = jnp.where(incl, jnp.exp(jnp.where(incl, diff, 0.0)), 0.0)
    dec_strict = jnp.where(strict, dec_incl, 0.0)
    m = beta[..., :, None] * jnp.einsum('nzhid,nzhjd->nzhij', k, k) * dec_strict
    lhs = m + jnp.eye(cs, dtype=m.dtype)
    rhs = jnp.concatenate([beta[..., None] * v, (beta * jnp.exp(gc))[..., None] * k], axis=-1)
    sol = lax.linalg.triangular_solve(lhs, rhs, left_side=True, lower=True, unit_diagonal=True)
    u_v, w = sol[..., :dv], sol[..., dv:]
    qk = jnp.einsum('nzhid,nzhjd->nzhij', q, k) * dec_incl
    q_dec = q * jnp.exp(gc)[..., None]
    k_dec = k * jnp.exp(gc[..., -1:] - gc)[..., None]
    g_tot = jnp.exp(gc[..., -1])[..., None, None]

    def step(s, inp):
        u_v_n, w_n, qk_n, q_n, k_n, gt_n = inp
        u = u_v_n - jnp.einsum('zhck,zhkv->zhcv', w_n, s)
        o = jnp.einsum('zhck,zhkv->zhcv', q_n, s) + jnp.einsum('zhij,zhjv->zhiv', qk_n, u)
        s = gt_n * s + jnp.einsum('zhck,zhcv->zhkv', k_n, u)
        return s, o

    s_fin, o = lax.scan(step, s0, (u_v, w, qk, q_dec, k_dec, g_tot))
    o = jnp.moveaxis(jnp.moveaxis(o, 2, 3), 0, 1).reshape(z, length, h, dv)
    return o, s_fin


def rwkv7_scan(r, w, k, v, a, b, s0):
    def step(s, inp):
        r_t, w_t, k_t, v_t, a_t, b_t = inp
        sa = jnp.einsum('zhvk,zhk->zhv', s, a_t)
        s = s * w_t[:, :, None, :] + sa[..., None] * b_t[:, :, None, :] + v_t[..., None] * k_t[:, :, None, :]
        return s, jnp.einsum('zhvk,zhk->zhv', s, r_t)

    xs = tuple(jnp.moveaxis(t, 1, 0) for t in (r, w, k, v, a, b))
    s_fin, y = lax.scan(step, s0, xs)
    return jnp.moveaxis(y, 0, 1), s_fin


def linear_scan(a, b, h0):
    b = b.at[:, 0].add(a[:, 0] * h0)

    def comb(e1, e2):
        return e1[0] * e2[0], e2[0] * e1[1] + e2[1]

    _, h = lax.associative_scan(comb, (a, b), axis=1)
    return h


def even_mixer(h, w_in, w_out, dn_conv, dn_a_log, dn_dt_bias, dn_norm_g,
               rw_mu, rw_w0, rw_w2, rw_a0, rw_a2, rw_g2, rw_k_k, rw_k_a, rw_r_k, rw_ln_g, rw_ln_b,
               s_dn, s_rw):
    f32 = jnp.float32
    bsz, length, _ = h.shape
    p = h @ w_in
    p_a, p_b = p[..., :EV_A_IN], p[..., EV_A_IN:]
    qkv_raw, zg, b_raw, a_raw = split_sizes(p_a, EV_A_SIZES)
    qkv = jax.nn.silu(short_conv1d(qkv_raw, dn_conv, CONV_LEFT)).astype(f32)
    q, k, v = split_sizes(qkv, (DN_QK, DN_QK, DN_VW))
    q = l2norm(q.reshape(bsz, length, DN_HEADS, DN_DK)) * (DN_DK ** -0.5)
    k = l2norm(k.reshape(bsz, length, DN_HEADS, DN_DK))
    v = v.reshape(bsz, length, DN_HEADS, DN_DV)
    beta = jax.nn.sigmoid(b_raw.astype(f32)).reshape(bsz, length, N_DIR, DN_HEADS)
    g = -jnp.exp(dn_a_log.astype(f32)) * jax.nn.softplus(
        a_raw.astype(f32).reshape(bsz, length, N_DIR, DN_HEADS) + dn_dt_bias.astype(f32))
    o, s_dn_fin = gated_delta_chunked(
        dir_stack(q, q), dir_stack(k, k), dir_stack(v, v),
        dir_stack(beta[:, :, 0], beta[:, :, 1]), dir_stack(g[:, :, 0], g[:, :, 1]),
        s_dn.astype(f32).reshape(bsz * N_DIR, DN_HEADS, DN_DK, DN_DV))
    o = dir_merge(o, bsz)
    o = rms_norm(o, dn_norm_g) * jax.nn.silu(zg.astype(f32)).reshape(bsz, length, DN_HEADS, DN_DV)
    o_a = o.reshape(bsz, length, DN_VW)
    xs = token_shift(p_b, rw_mu[0], rw_mu[1])
    r, kr, vr, wd, ad, gd = split_sizes(xs.astype(f32), EV_B_SIZES)
    wd = wd.reshape(bsz, length, N_DIR, RW_R_DECAY)
    ad = ad.reshape(bsz, length, N_DIR, RW_R_A)
    w_pre = rw_w0 + jnp.einsum('bldr,drc->bldc', jnp.tanh(wd), rw_w2)
    decay = jnp.exp(-jnp.exp(-jax.nn.softplus(-w_pre) - 0.5))
    a_g = jax.nn.sigmoid(rw_a0 + jnp.einsum('bldr,drc->bldc', ad, rw_a2))
    gate = jax.nn.sigmoid(gd) @ rw_g2
    kk = l2norm((kr * rw_k_k).reshape(bsz, length, RW_HEADS, RW_N)).reshape(bsz, length, RW_C)
    k_dir = kr[:, :, None, :] * (1.0 + (a_g - 1.0) * rw_k_a)

    def heads(t):
        return t.reshape(t.shape[:2] + (RW_HEADS, RW_N))

    y, s_rw_fin = rwkv7_scan(
        heads(dir_stack(r, r)), heads(dir_stack(decay[:, :, 0], decay[:, :, 1])),
        heads(dir_stack(k_dir[:, :, 0], k_dir[:, :, 1])), heads(dir_stack(vr, vr)),
        heads(dir_stack(-kk, -kk)), heads(dir_stack(kk * a_g[:, :, 0], kk * a_g[:, :, 1])),
        s_rw.astype(f32).reshape(bsz * N_DIR, RW_HEADS, RW_N, RW_N))
    y = dir_merge(y, bsz)
    mean = jnp.mean(y, axis=-1, keepdims=True)
    var = jnp.mean(jnp.square(y - mean), axis=-1, keepdims=True)
    yn = ((y - mean) * lax.rsqrt(var + RW_GN_EPS)).reshape(bsz, length, RW_C) * rw_ln_g + rw_ln_b
    rk = jnp.sum(r[:, :, None, :] * k_dir, axis=2).reshape(bsz, length, RW_HEADS, RW_N)
    bonus = (jnp.sum(rk * rw_r_k, axis=-1, keepdims=True) * heads(vr)).reshape(bsz, length, RW_C)
    o_b = (yn + bonus) * gate
    out = jnp.concatenate([o_a, o_b], axis=-1).astype(h.dtype) @ w_out
    return (out, s_dn_fin.reshape(bsz, N_DIR, DN_HEADS, DN_DK, DN_DV),
            s_rw_fin.reshape(bsz, N_DIR, RW_HEADS, RW_N, RW_N))


def odd_mixer(h, w_in, w_out, conv_w, conv_b, w_gate, b_gate, lam, s_lru):
    f32 = jnp.float32
    bsz, length, _ = h.shape
    p = h @ w_in
    gate_br, x_br = p[..., :LRU_W], p[..., LRU_W:]
    xc = (short_conv1d(x_br, conv_w, CONV_LEFT) + conv_b).astype(f32)
    xb = xc.reshape(bsz, length, LRU_BLOCKS, LRU_BS)
    gates = jnp.einsum('blnj,dgnjk->bldgnk', xb, w_gate.astype(f32)).reshape(
        bsz, length, N_DIR, 2, LRU_W) + b_gate.astype(f32)
    r_g = jax.nn.sigmoid(gates[..., 0, :])
    i_g = jax.nn.sigmoid(gates[..., 1, :])
    log_a = -LRU_C * r_g * jax.nn.softplus(-lam.astype(f32))
    a = jnp.exp(log_a)
    bterm = jnp.sqrt(-jnp.expm1(2.0 * log_a)) * (i_g * xc[:, :, None, :])
    hz = linear_scan(dir_stack(a[:, :, 0], a[:, :, 1]), dir_stack(bterm[:, :, 0], bterm[:, :, 1]),
                     s_lru.astype(f32).reshape(bsz * N_DIR, LRU_W))
    y = dir_merge(hz, bsz)
    s_fin = hz[:, -1].reshape(bsz, N_DIR, LRU_W)
    out = (jax.nn.gelu(gate_br) * y.astype(h.dtype)) @ w_out
    return out, s_fin


def conv_ffn(h, w_up, w_conv, w_down, rows):
    bsz, length, _ = h.shape
    u = (h @ w_up).reshape(bsz, rows, length // rows, 2 * D_FF)
    u = lax.conv_general_dilated(u, w_conv[:, :, None, :], window_strides=(1, 1), padding='SAME',
                                 dimension_numbers=('NHWC', 'HWIO', 'NHWC'),
                                 feature_group_count=2 * D_FF)
    u = u.reshape(bsz, length, 2 * D_FF)
    return (jax.nn.silu(u[..., :D_FF]) * u[..., D_FF:]) @ w_down


def modulation(cvec, w_mod, b_mod):
    m = jax.nn.silu(cvec) @ w_mod + b_mod
    return jnp.split(m[:, None, :], N_MOD, axis=-1)


def setup_inputs(seed: int = 0) -> dict:
    key = jax.random.key(seed)
    ks = iter(jax.random.split(key, 48))
    f32 = jnp.float32

    def nrm(shape, scale):
        return scale * jax.random.normal(next(ks), shape, f32)

    def uni(shape, lo, hi):
        return jax.random.uniform(next(ks), shape, f32, lo, hi)

    d = D_MODEL
    dt = jnp.exp(uni((N_EVEN, N_DIR, DN_HEADS), math.log(1e-3), math.log(0.1)))
    a_lru = uni((N_ODD, N_DIR, LRU_W), 0.9, 0.999) ** (1.0 / LRU_C)
    return {
        'x_prompt': nrm((BATCH, SEQ, d), 1.0),
        'x_sample': nrm((DEC_BATCH, DEC_SEQ, d), 1.0),
        'state_dn': nrm((DEC_BATCH, N_EVEN, N_DIR, DN_HEADS, DN_DK, DN_DV), 0.1),
        'state_rwkv': nrm((DEC_BATCH, N_EVEN, N_DIR, RW_HEADS, RW_N, RW_N), 0.5),
        'state_lru': nrm((DEC_BATCH, N_ODD, N_DIR, LRU_W), 0.5),
        'c': nrm((DEC_BATCH, d), 1.0),
        'c_ctx': nrm((d,), 1.0),
        'w_mod': nrm((DEPTH, d, N_MOD * d), 0.5 * d ** -0.5),
        'b_mod': nrm((DEPTH, N_MOD * d), 0.02),
        'norm_g': 1.0 + nrm((DEPTH, 4, d), 0.05),
        'ffn_w_up': nrm((DEPTH, d, 2 * D_FF), d ** -0.5),
        'ffn_conv': nrm((DEPTH, FFN_K, FFN_K, 2 * D_FF), 1.0 / FFN_K),
        'ffn_w_down': nrm((DEPTH, D_FF, d), D_FF ** -0.5),
        'ev_w_in': nrm((N_EVEN, d, EV_IN), d ** -0.5),
        'ev_w_out': nrm((N_EVEN, EV_OUT, d), EV_OUT ** -0.5),
        'dn_conv': nrm((N_EVEN, DN_CONV, 2 * DN_QK + DN_VW), 0.5),
        'dn_a_log': jnp.log(uni((N_EVEN, N_DIR, DN_HEADS), 1.0, 16.0)),
        'dn_dt_bias': dt + jnp.log(-jnp.expm1(-dt)),
        'dn_norm_g': 1.0 + nrm((N_EVEN, DN_DV), 0.05),
        'rw_mu': uni((N_EVEN, 2, EV_B_IN), 0.0, 0.5),
        'rw_w0': uni((N_EVEN, N_DIR, RW_C), -4.0, 1.0),
        'rw_w2': nrm((N_EVEN, N_DIR, RW_R_DECAY, RW_C), 0.1),
        'rw_a0': nrm((N_EVEN, N_DIR, RW_C), 0.1),
        'rw_a2': nrm((N_EVEN, N_DIR, RW_R_A, RW_C), RW_R_A ** -0.5),
        'rw_g2': nrm((N_EVEN, RW_R_GATE, RW_C), RW_R_GATE ** -0.5),
        'rw_k_k': 0.85 + nrm((N_EVEN, RW_C), 0.05),
        'rw_k_a': 1.0 + nrm((N_EVEN, RW_C), 0.05),
        'rw_r_k': nrm((N_EVEN, RW_HEADS, RW_N), 0.1),
        'rw_ln_g': 1.0 + nrm((N_EVEN, RW_C), 0.05),
        'rw_ln_b': nrm((N_EVEN, RW_C), 0.02),
        'od_w_in': nrm((N_ODD, d, OD_IN), d ** -0.5),
        'od_w_out': nrm((N_ODD, LRU_W, d), LRU_W ** -0.5),
        'lru_conv': nrm((N_ODD, LRU_CONV, LRU_W), 0.5),
        'lru_conv_b': nrm((N_ODD, LRU_W), 0.02),
        'lru_w_gate': nrm((N_ODD, N_DIR, 2, LRU_BLOCKS, LRU_BS, LRU_BS), LRU_BS ** -0.5),
        'lru_b_gate': nrm((N_ODD, N_DIR, 2, LRU_W), 0.02),
        'lru_lambda': jnp.log(a_lru) - jnp.log1p(-a_lru),
    }


def reference(x_prompt, x_sample, state_dn, state_rwkv, state_lru, c, c_ctx,
              w_mod, b_mod, norm_g, ffn_w_up, ffn_conv, ffn_w_down,
              ev_w_in, ev_w_out, dn_conv, dn_a_log, dn_dt_bias, dn_norm_g,
              rw_mu, rw_w0, rw_w2, rw_a0, rw_a2, rw_g2, rw_k_k, rw_k_a, rw_r_k, rw_ln_g, rw_ln_b,
              od_w_in, od_w_out, lru_conv, lru_conv_b, lru_w_gate, lru_b_gate, lru_lambda):
    def trunk(x, cvec, rows, s_dn, s_rw, s_lru):
        fin_dn, fin_rw, fin_lru = [], [], []
        for l in range(DEPTH):
            sh_m, sc_m, gt_m, sh_f, sc_f, gt_f = modulation(cvec, w_mod[l], b_mod[l])
            h = rms_norm(x, norm_g[l, 0]) * (1.0 + sc_m) + sh_m
            i = l // 2
            if l % 2 == 0:
                mix, sd, sr = even_mixer(
                    h, ev_w_in[i], ev_w_out[i], dn_conv[i], dn_a_log[i], dn_dt_bias[i], dn_norm_g[i],
                    rw_mu[i], rw_w0[i], rw_w2[i], rw_a0[i], rw_a2[i], rw_g2[i], rw_k_k[i], rw_k_a[i],
                    rw_r_k[i], rw_ln_g[i], rw_ln_b[i], s_dn[:, i], s_rw[:, i])
                fin_dn.append(sd)
                fin_rw.append(sr)
            else:
                mix, sl = odd_mixer(h, od_w_in[i], od_w_out[i], lru_conv[i], lru_conv_b[i],
                                    lru_w_gate[i], lru_b_gate[i], lru_lambda[i], s_lru[:, i])
                fin_lru.append(sl)
            x = x + gt_m * rms_norm(mix, norm_g[l, 1])
            h = rms_norm(x, norm_g[l, 2]) * (1.0 + sc_f) + sh_f
            x = x + gt_f * rms_norm(conv_ffn(h, ffn_w_up[l], ffn_conv[l], ffn_w_down[l], rows), norm_g[l, 3])
        return x, jnp.stack(fin_dn, axis=1), jnp.stack(fin_rw, axis=1), jnp.stack(fin_lru, axis=1)

    f32 = jnp.float32
    bp = x_prompt.shape[0]
    y_prompt, new_dn, new_rw, new_lru = trunk(
        x_prompt, c_ctx[None, :], 1,
        jnp.zeros((bp, N_EVEN, N_DIR, DN_HEADS, DN_DK, DN_DV), f32),
        jnp.zeros((bp, N_EVEN, N_DIR, RW_HEADS, RW_N, RW_N), f32),
        jnp.zeros((bp, N_ODD, N_DIR, LRU_W), f32))
    rows = x_sample.shape[1] // GRID_W
    y_sample, _, _, _ = trunk(x_sample, c, rows, state_dn, state_rwkv, state_lru)
    return (y_prompt, y_sample, new_dn, new_rw, new_lru)
```

```python
import functools
import math

import jax
import jax.numpy as jnp
from jax import lax
from jax.experimental import pallas as pl
from jax.experimental.pallas import tpu as pltpu

F32 = jnp.float32
BF16 = jnp.bfloat16
HIGHEST = lax.Precision.HIGHEST

NORM_EPS = 1e-6
CHUNK = 64
VMEM_LIMIT_BYTES = 48 * 1024 * 1024


def _cparams(*sem):
    return pltpu.CompilerParams(dimension_semantics=sem, vmem_limit_bytes=VMEM_LIMIT_BYTES)


def _dotb(a, b):
    return jnp.dot(a.astype(BF16), b.astype(BF16), preferred_element_type=F32)


def _dotb_nt(a, b):
    return lax.dot_general(a.astype(BF16), b.astype(BF16), (((1,), (1,)), ((), ())),
                           preferred_element_type=F32)


def _dotb_tn(a, b):
    return lax.dot_general(a.astype(BF16), b.astype(BF16), (((0,), (0,)), ((), ())),
                           preferred_element_type=F32)


def _dot_hi(a, b):
    return jnp.dot(a, b, preferred_element_type=F32, precision=HIGHEST)


def _mm_body(a_ref, b_ref, o_ref):
    o_ref[...] = jnp.dot(a_ref[...], b_ref[...], preferred_element_type=F32).astype(o_ref.dtype)


def matmul(a, b, out_dtype=F32, tm=512, tn=512):
    m, k = a.shape
    n = b.shape[1]
    tm, tn = min(tm, m), min(tn, n)
    assert m % tm == 0 and n % tn == 0, (a.shape, b.shape)
    return pl.pallas_call(
        _mm_body,
        grid=(m // tm, n // tn),
        in_specs=[pl.BlockSpec((tm, k), lambda i, j: (i, 0)),
                  pl.BlockSpec((k, tn), lambda i, j: (0, j))],
        out_specs=pl.BlockSpec((tm, tn), lambda i, j: (i, j)),
        out_shape=jax.ShapeDtypeStruct((m, n), out_dtype),
        compiler_params=_cparams("parallel", "arbitrary"),
    )(a.astype(BF16), b.astype(BF16))


def _order_masks(c, rev):
    ii = lax.broadcasted_iota(jnp.int32, (c, c), 0)
    jj = lax.broadcasted_iota(jnp.int32, (c, c), 1)
    lo = jnp.where(rev, jj, ii)
    hi = jnp.where(rev, ii, jj)
    return ii, jj, lo, hi


def _unit_tri_inverse(m, ii, jj, lo, hi):
    c = m.shape[0]
    eye = (ii == jj).astype(F32)

    def pair_mask(shift):
        same = (lo >> (shift + 1)) == (hi >> (shift + 1))
        return same & (((lo >> shift) & 1) == 1) & (((hi >> shift) & 1) == 0)

    t = eye - jnp.where(pair_mask(0), m, 0.0)
    shift = 1
    while (1 << shift) < c:
        b = jnp.where(pair_mask(shift), m, 0.0)
        t = t - _dot_hi(_dot_hi(t, b), t)
        shift += 1
    return t


def _delta_body(q_ref, k_ref, v_ref, beta_ref, g_ref, gt_ref, s0_ref, o_ref, s_ref, *, heads, dk, dv):
    d = pl.program_id(1)
    n = pl.program_id(2)
    rev = d == 1
    c = q_ref.shape[1]

    @pl.when(n == 0)
    def _():
        s_ref[...] = s0_ref[...]

    ii, jj, lo, hi = _order_masks(c, rev)
    incl = lo >= hi
    strict = lo > hi
    m_incl = incl.astype(F32)

    g = g_ref[0, 0]
    beta = beta_ref[0, 0]
    gc_col = _dot_hi(m_incl, g)
    gc_row = lax.dot_general(gt_ref[0, 0, 0], m_incl, (((1,), (1,)), ((), ())),
                             preferred_element_type=F32, precision=HIGHEST)
    g_tot = jnp.sum(g, axis=0, keepdims=True)

    for h in range(heads):
        q = q_ref[0, :, h * dk:(h + 1) * dk]
        k = k_ref[0, :, h * dk:(h + 1) * dk]
        v = v_ref[0, :, h * dv:(h + 1) * dv]
        s = s_ref[0, 0, h]
        gcc = gc_col[:, h:h + 1]
        gcr = gc_row[h:h + 1, :]
        bc = beta[:, h:h + 1]
        gt = g_tot[:, h:h + 1]
        diff = gcc - gcr
        dec_incl = jnp.where(incl, jnp.exp(jnp.where(incl, diff, 0.0)), 0.0)
        dec_strict = jnp.where(strict, dec_incl, 0.0)
        kk = _dotb_nt(k, k)
        qk = _dotb_nt(q, k) * dec_incl
        m = bc * kk * dec_strict
        t = _unit_tri_inverse(m, ii, jj, lo, hi)
        egc = jnp.exp(gcc)
        rhs = jnp.concatenate([bc * v, (bc * egc) * k], axis=1)
        sol = _dotb(t, rhs)
        u = sol[:, :dv] - _dotb(sol[:, dv:], s)
        o = _dotb(q * egc, s) + _dotb(qk, u)
        k_dec = k * jnp.exp(gt - gcc)
        s_ref[0, 0, h] = jnp.exp(gt) * s + _dotb_tn(k_dec, u)
        o_ref[0, 0, :, h * dv:(h + 1) * dv] = o


def delta_scan(q, k, v, beta, g, s0):
    bsz, length, _ = q.shape
    _, _, heads, dk, dv = s0.shape
    c = CHUNK
    n = length // c
    gt = jnp.swapaxes(g.reshape(bsz, 2, n, c, heads), 3, 4)

    def chunk_of(d, i):
        return i + d * (n - 1 - 2 * i)

    tok = lambda width: pl.BlockSpec((1, c, width), lambda b, d, i: (b, chunk_of(d, i), 0))
    par = pl.BlockSpec((1, 1, c, heads), lambda b, d, i: (b, d, chunk_of(d, i), 0))
    st = pl.BlockSpec((1, 1, heads, dk, dv), lambda b, d, i: (b, d, 0, 0, 0))
    return pl.pallas_call(
        functools.partial(_delta_body, heads=heads, dk=dk, dv=dv),
        grid=(bsz, 2, n),
        in_specs=[tok(heads * dk), tok(heads * dk), tok(heads * dv), par, par,
                  pl.BlockSpec((1, 1, 1, heads, c), lambda b, d, i: (b, d, chunk_of(d, i), 0, 0)),
                  st],
        out_specs=[pl.BlockSpec((1, 1, c, heads * dv), lambda b, d, i: (b, d, chunk_of(d, i), 0)), st],
        out_shape=[jax.ShapeDtypeStruct((bsz, 2, length, heads * dv), F32),
                   jax.ShapeDtypeStruct(s0.shape, F32)],
        compiler_params=_cparams("parallel", "parallel", "arbitrary"),
    )(q, k, v, beta, g, gt, s0)


def _rwkv_body(r_ref, v_ref, kk_ref, lw_ref, kd_ref, ag_ref, s0_ref, y_ref, s_ref, *, heads, hn):
    d = pl.program_id(1)
    n = pl.program_id(2)
    rev = d == 1
    c = r_ref.shape[1]

    @pl.when(n == 0)
    def _():
        s_ref[...] = s0_ref[...]

    ii, jj, lo, hi = _order_masks(c, rev)
    incl = lo >= hi
    strict = lo > hi
    m_incl = incl.astype(F32)

    logw = lw_ref[0, 0]
    lw = _dot_hi(m_incl, logw)
    lw_tot = jnp.sum(logw, axis=0, keepdims=True)
    r = r_ref[0]
    v = v_ref[0]
    kk = kk_ref[0]
    kd = kd_ref[0, 0]
    bb = kk * ag_ref[0, 0]
    e_out = jnp.exp(-lw)
    e_rem = jnp.exp(lw_tot - lw)
    a_t = -kk * jnp.exp(lw - logw)
    r_t = r * jnp.exp(lw)
    b_t = bb * e_out
    k_t = kd * e_out
    b_p = bb * e_rem
    k_p = kd * e_rem
    p_c = jnp.exp(lw_tot)

    for h in range(heads):
        sl = slice(h * hn, (h + 1) * hn)
        s = s_ref[0, 0, h]
        lhs = jnp.concatenate([a_t[:, sl], r_t[:, sl]], axis=0)
        rhs = jnp.concatenate([b_t[:, sl], k_t[:, sl]], axis=0)
        gram = _dotb_nt(lhs, rhs)
        a_ab = jnp.where(strict, gram[:c, :c], 0.0)
        a_ak = jnp.where(strict, gram[:c, c:], 0.0)
        r_bk = jnp.concatenate([jnp.where(incl, gram[c:, :c], 0.0),
                                jnp.where(incl, gram[c:, c:], 0.0)], axis=1)
        t = _unit_tri_inverse(-a_ab, ii, jj, lo, hi)
        ls = _dotb_nt(lhs, s)
        vh = v[:, sl]
        u = _dotb(t, ls[:c] + _dotb(a_ak, vh))
        uv = jnp.concatenate([u, vh], axis=0)
        y = ls[c:] + _dotb(r_bk, uv)
        bk = jnp.concatenate([b_p[:, sl], k_p[:, sl]], axis=0)
        s_ref[0, 0, h] = s * p_c[:, sl] + _dotb_tn(uv, bk)
        y_ref[0, 0, :, sl] = y


def rwkv_scan(r, v, kk, logw, kdir, ag, s0):
    bsz, length, width = r.shape
    _, _, heads, hn, _ = s0.shape
    c = CHUNK
    n = length // c

    def chunk_of(d, i):
        return i + d * (n - 1 - 2 * i)

    tok = pl.BlockSpec((1, c, width), lambda b, d, i: (b, chunk_of(d, i), 0))
    par = pl.BlockSpec((1, 1, c, width), lambda b, d, i: (b, d, chunk_of(d, i), 0))
    st = pl.BlockSpec((1, 1, heads, hn, hn), lambda b, d, i: (b, d, 0, 0, 0))
    return pl.pallas_call(
        functools.partial(_rwkv_body, heads=heads, hn=hn),
        grid=(bsz, 2, n),
        in_specs=[tok, tok, tok, par, par, par, st],
        out_specs=[par, st],
        out_shape=[jax.ShapeDtypeStruct((bsz, 2, length, width), F32),
                   jax.ShapeDtypeStruct(s0.shape, F32)],
        compiler_params=_cparams("parallel", "parallel", "arbitrary"),
    )(r, v, kk, logw, kdir, ag, s0)


LRU_C = 8.0


def _scan_pass(a, b, shift, rows, rev):
    length = a.shape[0]
    if rev:
        a_s = pltpu.roll(a, length - shift, 0)
        b_s = pltpu.roll(b, length - shift, 0)
        valid = rows < length - shift
    else:
        a_s = pltpu.roll(a, shift, 0)
        b_s = pltpu.roll(b, shift, 0)
        valid = rows >= shift
    a_s = jnp.where(valid, a_s, 1.0)
    b_s = jnp.where(valid, b_s, 0.0)
    return a * a_s, a * b_s + b


def _lru_body(x_ref, gbr_ref, wg_ref, bg_ref, lam_ref, s0_ref, o_ref, s_ref):
    x = x_ref[0]
    length = x.shape[0]
    rows = lax.broadcasted_iota(jnp.int32, x.shape, 0)
    xb = x.astype(BF16)
    y = jnp.zeros_like(x)
    for d in range(2):
        rev = d == 1
        r_g = jax.nn.sigmoid(jnp.dot(xb, wg_ref[d, 0, 0].astype(BF16), preferred_element_type=F32)
                             + bg_ref[d, 0:1, :])
        i_g = jax.nn.sigmoid(jnp.dot(xb, wg_ref[d, 1, 0].astype(BF16), preferred_element_type=F32)
                             + bg_ref[d, 1:2, :])
        lam = lam_ref[d:d + 1, :]
        log_a = -LRU_C * r_g * jax.nn.softplus(-lam)
        a = jnp.exp(log_a)
        b = jnp.sqrt(-jnp.tanh(log_a) * (a * a + 1.0)) * (i_g * x)
        first = (rows == (length - 1 if rev else 0))
        b = jnp.where(first, b + a * s0_ref[0, d:d + 1, :], b)
        shift = 1
        while shift < length:
            a, b = _scan_pass(a, b, shift, rows, rev)
            shift *= 2
        y = y + b
        s_ref[0, d:d + 1, :] = b[0:1, :] if rev else b[length - 1:length, :]
    o_ref[0] = (jax.nn.gelu(gbr_ref[0]) * y).astype(o_ref.dtype)


def lru_mix(xc, gate_br, w_gate, b_gate, lam, s0):
    bsz, length, width = xc.shape
    nb, bs = w_gate.shape[2], w_gate.shape[3]
    tok = pl.BlockSpec((1, length, bs), lambda b, j: (b, 0, j))
    st = pl.BlockSpec((1, 2, bs), lambda b, j: (b, 0, j))
    return pl.pallas_call(
        _lru_body,
        grid=(bsz, nb),
        in_specs=[tok, tok,
                  pl.BlockSpec((2, 2, 1, bs, bs), lambda b, j: (0, 0, j, 0, 0)),
                  pl.BlockSpec((2, 2, bs), lambda b, j: (0, 0, j)),
                  pl.BlockSpec((2, bs), lambda b, j: (0, j)),
                  st],
        out_specs=[tok, st],
        out_shape=[jax.ShapeDtypeStruct((bsz, length, width), BF16),
                   jax.ShapeDtypeStruct((bsz, 2, width), F32)],
        compiler_params=_cparams("parallel", "parallel"),
    )(xc, gate_br, w_gate, b_gate, lam, s0)


def _dwconv(x, w_ref, cols, grid_w, image_rows):
    tokens = x.shape[0]
    w = lambda i, j: w_ref[3 * i + j:3 * i + j + 1, :]
    x_l = pltpu.roll(jnp.where(cols != grid_w - 1, x, 0.0), 1, 0)
    x_r = pltpu.roll(jnp.where(cols != 0, x, 0.0), tokens - 1, 0)
    out = w(1, 0) * x_l + w(1, 1) * x + w(1, 2) * x_r
    if image_rows > 1:
        assert tokens == grid_w * image_rows
        up = w(0, 0) * x_l + w(0, 1) * x + w(0, 2) * x_r
        dn = w(2, 0) * x_l + w(2, 1) * x + w(2, 2) * x_r
        zero = jnp.zeros((grid_w, x.shape[1]), x.dtype)
        out = out + jnp.concatenate([zero, up[:tokens - grid_w]], axis=0)
        out = out + jnp.concatenate([dn[grid_w:], zero], axis=0)
    return out


def _ffn_mid_body(u1_ref, u2_ref, w1_ref, w2_ref, o_ref, *, grid_w, image_rows):
    cols = lax.broadcasted_iota(jnp.int32, u1_ref.shape, 0) & (grid_w - 1)
    c1 = _dwconv(u1_ref[...], w1_ref, cols, grid_w, image_rows)
    c2 = _dwconv(u2_ref[...], w2_ref, cols, grid_w, image_rows)
    o_ref[...] = (jax.nn.silu(c1) * c2).astype(o_ref.dtype)


def ffn_mid(u, w_conv, grid_w, image_rows, tb=1024, tc=256):
    t, f2 = u.shape
    f = f2 // 2
    nf = f // tc
    assert t % tb == 0 and f % tc == 0 and tb % (grid_w * image_rows) == 0
    assert image_rows == 1 or tb == grid_w * image_rows
    return pl.pallas_call(
        functools.partial(_ffn_mid_body, grid_w=grid_w, image_rows=image_rows),
        grid=(t // tb, nf),
        in_specs=[pl.BlockSpec((tb, tc), lambda i, j: (i, j)),
                  pl.BlockSpec((tb, tc), lambda i, j: (i, j + nf)),
                  pl.BlockSpec((9, tc), lambda i, j: (0, j)),
                  pl.BlockSpec((9, tc), lambda i, j: (0, j + nf))],
        out_specs=pl.BlockSpec((tb, tc), lambda i, j: (i, j)),
        out_shape=jax.ShapeDtypeStruct((t, f), BF16),
        compiler_params=_cparams("parallel", "parallel"),
    )(u, u, w_conv, w_conv)


def _rms_norm(x, g):
    return x * lax.rsqrt(jnp.mean(x * x, axis=-1, keepdims=True) + NORM_EPS) * g


def _l2norm(x):
    return x * lax.rsqrt(jnp.sum(x * x, axis=-1, keepdims=True) + 1e-6)


def _short_conv1d(x, w, left):
    k, length = w.shape[0], x.shape[1]
    xp = jnp.pad(x, ((0, 0), (left, k - 1 - left), (0, 0)))
    out = xp[:, 0:length] * w[0]
    for j in range(1, k):
        out = out + xp[:, j:j + length] * w[j]
    return out


def _token_shift(x, mu_prev, mu_next):
    prev = jnp.pad(x, ((0, 0), (1, 0), (0, 0)))[:, :-1]
    nxt = jnp.pad(x, ((0, 0), (0, 1), (0, 0)))[:, 1:]
    return x + mu_prev * (prev - x) + mu_next * (nxt - x)


def _dir_major(x):
    return jnp.swapaxes(x, 1, 2)


def _even_mixer(h, w_in, w_out, dn_conv, dn_a_log, dn_dt_bias, dn_norm_g,
                rw_mu, rw_w0, rw_w2, rw_a0, rw_a2, rw_g2, rw_k_k, rw_k_a, rw_r_k, rw_ln_g, rw_ln_b,
                s_dn, s_rw):
    bsz, length, dm = h.shape
    dn_heads, dn_dk, dn_dv = s_dn.shape[2:]
    rw_heads, rw_n = s_rw.shape[2:4]
    dn_qk, dn_vw, rw_c = dn_heads * dn_dk, dn_heads * dn_dv, rw_heads * rw_n
    r_decay, r_a, r_gate = rw_w2.shape[1], rw_a2.shape[1], rw_g2.shape[0]
    p = matmul(h.reshape(bsz * length, dm), _pad_cols(w_in, 512))[:, :w_in.shape[1]]
    p = p.reshape(bsz, length, -1)
    ev_a = 2 * dn_qk + 2 * dn_vw + 4 * dn_heads
    p_a, p_b = p[..., :ev_a], p[..., ev_a:]
    qkv_raw = p_a[..., :2 * dn_qk + dn_vw]
    zg = p_a[..., 2 * dn_qk + dn_vw:2 * dn_qk + 2 * dn_vw]
    b_raw = p_a[..., 2 * dn_qk + 2 * dn_vw:2 * dn_qk + 2 * dn_vw + 2 * dn_heads]
    a_raw = p_a[..., 2 * dn_qk + 2 * dn_vw + 2 * dn_heads:]
    qkv = jax.nn.silu(_short_conv1d(qkv_raw, dn_conv, 2))
    q, k, v = qkv[..., :dn_qk], qkv[..., dn_qk:2 * dn_qk], qkv[..., 2 * dn_qk:]
    q = (_l2norm(q.reshape(bsz, length, dn_heads, dn_dk)) * (dn_dk ** -0.5)).reshape(bsz, length, dn_qk)
    k = _l2norm(k.reshape(bsz, length, dn_heads, dn_dk)).reshape(bsz, length, dn_qk)
    beta = jax.nn.sigmoid(b_raw).reshape(bsz, length, 2, dn_heads)
    g = -jnp.exp(dn_a_log) * jax.nn.softplus(a_raw.reshape(bsz, length, 2, dn_heads) + dn_dt_bias)
    o, s_dn_fin = delta_scan(q, k, v, _dir_major(beta), _dir_major(g), s_dn)
    o = (o[:, 0] + o[:, 1]).reshape(bsz, length, dn_heads, dn_dv)
    o = _rms_norm(o, dn_norm_g) * jax.nn.silu(zg).reshape(bsz, length, dn_heads, dn_dv)
    o_a = o.reshape(bsz, length, dn_vw)
    xs = _token_shift(p_b, rw_mu[0], rw_mu[1])
    r, kr, vr = xs[..., :rw_c], xs[..., rw_c:2 * rw_c], xs[..., 2 * rw_c:3 * rw_c]
    off = 3 * rw_c
    wd = xs[..., off:off + 2 * r_decay].reshape(bsz * length, 2, r_decay)
    ad = xs[..., off + 2 * r_decay:off + 2 * r_decay + 2 * r_a].reshape(bsz * length, 2, r_a)
    gd = xs[..., off + 2 * r_decay + 2 * r_a:].reshape(bsz * length, r_gate)
    w_pre = jnp.stack([matmul(jnp.tanh(wd[:, i]), rw_w2[i]) for i in range(2)], axis=1) + rw_w0
    log_decay = -jnp.exp(-jax.nn.softplus(-w_pre) - 0.5)
    a_g = jax.nn.sigmoid(jnp.stack([matmul(ad[:, i], rw_a2[i]) for i in range(2)], axis=1) + rw_a0)
    gate = matmul(jax.nn.sigmoid(gd), rw_g2).reshape(bsz, length, rw_c)
    kk = _l2norm((kr * rw_k_k).reshape(bsz, length, rw_heads, rw_n)).reshape(bsz, length, rw_c)
    a_g = a_g.reshape(bsz, length, 2, rw_c)
    log_decay = log_decay.reshape(bsz, length, 2, rw_c)
    k_dir = kr[:, :, None, :] * (1.0 + (a_g - 1.0) * rw_k_a)
    y, s_rw_fin = rwkv_scan(r, vr, kk, _dir_major(log_decay), _dir_major(k_dir), _dir_major(a_g), s_rw)
    y = (y[:, 0] + y[:, 1]).reshape(bsz, length, rw_heads, rw_n)
    mean = jnp.mean(y, axis=-1, keepdims=True)
    var = jnp.mean(jnp.square(y - mean), axis=-1, keepdims=True)
    yn = ((y - mean) * lax.rsqrt(var + rw_n * 1e-5)).reshape(bsz, length, rw_c) * rw_ln_g + rw_ln_b
    rk = jnp.sum(r[:, :, None, :] * k_dir, axis=2).reshape(bsz, length, rw_heads, rw_n)
    bonus = (jnp.sum(rk * rw_r_k, axis=-1, keepdims=True)
             * vr.reshape(bsz, length, rw_heads, rw_n)).reshape(bsz, length, rw_c)
    o_b = (yn + bonus) * gate
    cat = jnp.concatenate([o_a, o_b], axis=-1).reshape(bsz * length, dn_vw + rw_c)
    out = matmul(cat, w_out).reshape(bsz, length, dm)
    return out, s_dn_fin, s_rw_fin


def _pad_cols(w, mult):
    pad = (-w.shape[1]) % mult
    return jnp.pad(w, ((0, 0), (0, pad))) if pad else w


def _odd_mixer(h, w_in, w_out, conv_w, conv_b, w_gate, b_gate, lam, s_lru):
    bsz, length, dm = h.shape
    width = s_lru.shape[-1]
    p = matmul(h.reshape(bsz * length, dm), w_in).reshape(bsz, length, 2 * width)
    gate_br, x_br = p[..., :width], p[..., width:]
    xc = _short_conv1d(x_br, conv_w, 2) + conv_b
    a, s_fin = lru_mix(xc, gate_br, w_gate, b_gate, lam, s_lru)
    out = matmul(a.reshape(bsz * length, width), w_out).reshape(bsz, length, dm)
    return out, s_fin


def _conv_ffn(h, w_up, w_conv, w_down, rows):
    bsz, length, dm = h.shape
    u = matmul(h.reshape(bsz * length, dm), w_up)
    mid = ffn_mid(u, w_conv.reshape(9, -1), length // rows, rows)
    return matmul(mid, w_down).reshape(bsz, length, dm)


def _trunk(x, cvec, rows, s_dn, s_rw, s_lru, prm):
    depth = prm["w_mod"].shape[0]
    dm = x.shape[-1]
    fin_dn, fin_rw, fin_lru = [], [], []
    nc = cvec.shape[0]
    c_pad = jnp.pad(jax.nn.silu(cvec), ((0, (-nc) % 16), (0, 0)))
    for l in range(depth):
        m = matmul(c_pad, prm["w_mod"][l], tn=1024)[:nc] + prm["b_mod"][l]
        sh_m, sc_m, gt_m, sh_f, sc_f, gt_f = jnp.split(m[:, None, :], 6, axis=-1)
        g = prm["norm_g"][l]
        h = _rms_norm(x, g[0]) * (1.0 + sc_m) + sh_m
        i = l // 2
        if l % 2 == 0:
            mix, sd, sr = _even_mixer(
                h, prm["ev_w_in"][i], prm["ev_w_out"][i], prm["dn_conv"][i], prm["dn_a_log"][i],
                prm["dn_dt_bias"][i], prm["dn_norm_g"][i], prm["rw_mu"][i], prm["rw_w0"][i],
                prm["rw_w2"][i], prm["rw_a0"][i], prm["rw_a2"][i], prm["rw_g2"][i], prm["rw_k_k"][i],
                prm["rw_k_a"][i], prm["rw_r_k"][i], prm["rw_ln_g"][i], prm["rw_ln_b"][i],
                s_dn[:, i], s_rw[:, i])
            fin_dn.append(sd)
            fin_rw.append(sr)
        else:
            mix, sl = _odd_mixer(h, prm["od_w_in"][i], prm["od_w_out"][i], prm["lru_conv"][i],
                                 prm["lru_conv_b"][i], prm["lru_w_gate"][i], prm["lru_b_gate"][i],
                                 prm["lru_lambda"][i], s_lru[:, i])
            fin_lru.append(sl)
        x = x + gt_m * _rms_norm(mix, g[1])
        h = _rms_norm(x, g[2]) * (1.0 + sc_f) + sh_f
        ffn = _conv_ffn(h, prm["ffn_w_up"][l], prm["ffn_conv"][l], prm["ffn_w_down"][l], rows)
        x = x + gt_f * _rms_norm(ffn, g[3])
    return x, jnp.stack(fin_dn, axis=1), jnp.stack(fin_rw, axis=1), jnp.stack(fin_lru, axis=1)


def kernel(x_prompt, x_sample, state_dn, state_rwkv, state_lru, c, c_ctx, w_mod, b_mod, norm_g, ffn_w_up, ffn_conv, ffn_w_down, ev_w_in, ev_w_out, dn_conv, dn_a_log, dn_dt_bias, dn_norm_g, rw_mu, rw_w0, rw_w2, rw_a0, rw_a2, rw_g2, rw_k_k, rw_k_a, rw_r_k, rw_ln_g, rw_ln_b, od_w_in, od_w_out, lru_conv, lru_conv_b, lru_w_gate, lru_b_gate, lru_lambda):
    prm = dict(w_mod=w_mod, b_mod=b_mod, norm_g=norm_g, ffn_w_up=ffn_w_up, ffn_conv=ffn_conv,
               ffn_w_down=ffn_w_down, ev_w_in=ev_w_in, ev_w_out=ev_w_out, dn_conv=dn_conv,
               dn_a_log=dn_a_log, dn_dt_bias=dn_dt_bias, dn_norm_g=dn_norm_g, rw_mu=rw_mu, rw_w0=rw_w0,
               rw_w2=rw_w2, rw_a0=rw_a0, rw_a2=rw_a2, rw_g2=rw_g2, rw_k_k=rw_k_k, rw_k_a=rw_k_a,
               rw_r_k=rw_r_k, rw_ln_g=rw_ln_g, rw_ln_b=rw_ln_b, od_w_in=od_w_in, od_w_out=od_w_out,
               lru_conv=lru_conv, lru_conv_b=lru_conv_b, lru_w_gate=lru_w_gate, lru_b_gate=lru_b_gate,
               lru_lambda=lru_lambda)
    bp = x_prompt.shape[0]
    grid_w = 64
    y_prompt, new_dn, new_rw, new_lru = _trunk(
        x_prompt, c_ctx[None, :], 1,
        jnp.zeros((bp,) + state_dn.shape[1:], F32),
        jnp.zeros((bp,) + state_rwkv.shape[1:], F32),
        jnp.zeros((bp,) + state_lru.shape[1:], F32), prm)
    rows = x_sample.shape[1] // grid_w
    y_sample, _, _, _ = _trunk(x_sample, c, rows, state_dn, state_rwkv, state_lru, prm)
    return (y_prompt, y_sample, new_dn, new_rw, new_lru)
```

```python
import functools
import math

import jax
import jax.numpy as jnp
from jax import lax
from jax.experimental import pallas as pl
from jax.experimental.pallas import tpu as pltpu

F32 = jnp.float32
BF16 = jnp.bfloat16
HIGHEST = lax.Precision.HIGHEST

NORM_EPS = 1e-6
CHUNK = 64
VMEM_LIMIT_BYTES = 48 * 1024 * 1024


def _cparams(*sem):
    return pltpu.CompilerParams(dimension_semantics=sem, vmem_limit_bytes=VMEM_LIMIT_BYTES)


def _dotb(a, b):
    return jnp.dot(a.astype(BF16), b.astype(BF16), preferred_element_type=F32)


def _dotb_nt(a, b):
    return lax.dot_general(a.astype(BF16), b.astype(BF16), (((1,), (1,)), ((), ())),
                           preferred_element_type=F32)


def _dotb_tn(a, b):
    return lax.dot_general(a.astype(BF16), b.astype(BF16), (((0,), (0,)), ((), ())),
                           preferred_element_type=F32)


def _dot_hi(a, b):
    return jnp.dot(a, b, preferred_element_type=F32, precision=HIGHEST)


def _mm_body(a_ref, b_ref, o_ref):
    o_ref[...] = jnp.dot(a_ref[...], b_ref[...], preferred_element_type=F32).astype(o_ref.dtype)


LANES = 128


def _tile(n, target):
    if n <= target:
        return n
    best = None
    for t in range(LANES, target + 1, LANES):
        if n % t == 0:
            best = t
    assert best is not None, n
    return best


def matmul(a, b, out_dtype=F32, tm=1024, tn=1536):
    m, k = a.shape
    n = b.shape[1]
    tm, tn = _tile(m, tm), _tile(n, tn)
    return pl.pallas_call(
        _mm_body,
        grid=(m // tm, n // tn),
        in_specs=[pl.BlockSpec((tm, k), lambda i, j: (i, 0)),
                  pl.BlockSpec((k, tn), lambda i, j: (0, j))],
        out_specs=pl.BlockSpec((tm, tn), lambda i, j: (i, j)),
        out_shape=jax.ShapeDtypeStruct((m, n), out_dtype),
        compiler_params=_cparams("parallel", "arbitrary"),
    )(a.astype(BF16), b.astype(BF16))


def _mod_body(c_ref, w_ref, b_ref, o_ref):
    o_ref[0] = jnp.dot(c_ref[...], w_ref[0].astype(BF16), preferred_element_type=F32) + b_ref[0]


def modulation_all(cvec, w_mod, b_mod, tn=1024):
    nc, dm = cvec.shape
    depth, _, n = w_mod.shape
    return pl.pallas_call(
        _mod_body,
        grid=(depth, n // tn),
        in_specs=[pl.BlockSpec((nc, dm), lambda l, j: (0, 0)),
                  pl.BlockSpec((1, dm, tn), lambda l, j: (l, 0, j)),
                  pl.BlockSpec((1, 1, tn), lambda l, j: (l, 0, j))],
        out_specs=pl.BlockSpec((1, nc, tn), lambda l, j: (l, 0, j)),
        out_shape=jax.ShapeDtypeStruct((depth, nc, n), F32),
        compiler_params=_cparams("parallel", "parallel"),
    )(jax.nn.silu(cvec).astype(BF16), w_mod, b_mod.reshape(depth, 1, n))


def _norm_mod(x, g, scale, shift):
    return x * lax.rsqrt(jnp.mean(x * x, axis=-1, keepdims=True) + NORM_EPS) * g * (1.0 + scale) + shift


def _mod_spec(which, blocks_per_cond, dm):
    return pl.BlockSpec((1, 1, 1, dm), lambda i, *_: (i // blocks_per_cond, which, 0, 0))


def _norm_mod_body(x_ref, g_ref, sc_ref, sh_ref, h_ref):
    h_ref[...] = _norm_mod(x_ref[...], g_ref[...], sc_ref[0, 0], sh_ref[0, 0]).astype(h_ref.dtype)


def norm_mod(x, g, mod, which_shift, tm=512):
    t, dm = x.shape
    bpc = t // mod.shape[0] // tm
    row = pl.BlockSpec((tm, dm), lambda i: (i, 0))
    return pl.pallas_call(
        _norm_mod_body,
        grid=(t // tm,),
        in_specs=[row, pl.BlockSpec((1, dm), lambda i: (0, 0)),
                  _mod_spec(which_shift + 1, bpc, dm), _mod_spec(which_shift, bpc, dm)],
        out_specs=row,
        out_shape=jax.ShapeDtypeStruct((t, dm), BF16),
        compiler_params=_cparams("parallel"),
    )(x, g.reshape(1, dm), mod, mod)


def _mm_res_body(a_ref, w_ref, x_ref, g_ref, gate_ref, *rest, nk, emit_h):
    if emit_h:
        gn_ref, sc_ref, sh_ref, xo_ref, h_ref, acc_ref = rest
    else:
        xo_ref, acc_ref = rest
    k = pl.program_id(1)

    @pl.when(k == 0)
    def _():
        acc_ref[...] = jnp.zeros_like(acc_ref)

    acc_ref[...] += jnp.dot(a_ref[...], w_ref[...], preferred_element_type=F32)

    @pl.when(k == nk - 1)
    def _():
        y = acc_ref[...]
        y = y * lax.rsqrt(jnp.mean(y * y, axis=-1, keepdims=True) + NORM_EPS) * g_ref[...]
        xn = x_ref[...] + gate_ref[0, 0] * y
        xo_ref[...] = xn
        if emit_h:
            h_ref[...] = _norm_mod(xn, gn_ref[...], sc_ref[0, 0], sh_ref[0, 0]).astype(h_ref.dtype)


def matmul_residual(a, w, x, g_out, mod, which_gate, next_norm=None, tm=512, tk=1408):
    t, kdim = a.shape
    dm = w.shape[1]
    tk = _tile(kdim, tk)
    nk = kdim // tk
    bpc = t // mod.shape[0] // tm
    row = pl.BlockSpec((tm, dm), lambda i, k: (i, 0))
    vec = pl.BlockSpec((1, dm), lambda i, k: (0, 0))
    in_specs = [pl.BlockSpec((tm, tk), lambda i, k: (i, k)),
                pl.BlockSpec((tk, dm), lambda i, k: (k, 0)),
                row, vec, _mod_spec(which_gate, bpc, dm)]
    args = [a, w, x, g_out.reshape(1, dm), mod]
    out_specs = [row]
    out_shape = [jax.ShapeDtypeStruct((t, dm), F32)]
    if next_norm is not None:
        g_next, mod_next, which_shift = next_norm
        in_specs += [vec, _mod_spec(which_shift + 1, bpc, dm), _mod_spec(which_shift, bpc, dm)]
        args += [g_next.reshape(1, dm), mod_next, mod_next]
        out_specs.append(row)
        out_shape.append(jax.ShapeDtypeStruct((t, dm), BF16))
    out = pl.pallas_call(
        functools.partial(_mm_res_body, nk=nk, emit_h=next_norm is not None),
        grid=(t // tm, nk),
        in_specs=in_specs,
        out_specs=out_specs,
        out_shape=out_shape,
        scratch_shapes=[pltpu.VMEM((tm, dm), F32)],
        compiler_params=_cparams("parallel", "arbitrary"),
    )(*args)
    return out if next_norm is not None else (out[0], None)


def _order_masks(c, rev):
    ii = lax.broadcasted_iota(jnp.int32, (c, c), 0)
    jj = lax.broadcasted_iota(jnp.int32, (c, c), 1)
    lo = jnp.where(rev, jj, ii)
    hi = jnp.where(rev, ii, jj)
    return ii, jj, lo, hi


def _unit_tri_inverses(ms, ii, jj, lo, hi):
    c = ms[0].shape[0]
    eye = (ii == jj).astype(F32)

    def pair_mask(shift):
        same = (lo >> (shift + 1)) == (hi >> (shift + 1))
        return same & (((lo >> shift) & 1) == 1) & (((hi >> shift) & 1) == 0)

    mask = pair_mask(0)
    ts = [eye - jnp.where(mask, m, 0.0) for m in ms]
    shift = 1
    while (1 << shift) < c:
        mask = pair_mask(shift)
        tb = [_dotb(t, jnp.where(mask, m, 0.0)) for t, m in zip(ts, ms)]
        ts = [t - _dotb(x, t) for x, t in zip(tb, ts)]
        shift += 1
    return ts


def _delta_body(q_ref, k_ref, v_ref, beta_ref, g_ref, gt_ref, s0_ref, o_ref, s_ref, *, heads, dk, dv):
    d = pl.program_id(1)
    n = pl.program_id(2)
    rev = d == 1
    c = q_ref.shape[1]

    @pl.when(n == 0)
    def _():
        s_ref[...] = s0_ref[...]

    ii, jj, lo, hi = _order_masks(c, rev)
    incl = lo >= hi
    strict = lo > hi
    m_incl = incl.astype(F32)

    g = g_ref[0, 0]
    beta = beta_ref[0, 0]
    gc_col = _dot_hi(m_incl, g)
    gc_row = lax.dot_general(gt_ref[0, 0, 0], m_incl, (((1,), (1,)), ((), ())),
                             preferred_element_type=F32, precision=HIGHEST)
    g_tot = jnp.sum(g, axis=0, keepdims=True)

    hs = range(heads)
    q = [q_ref[0, :, h * dk:(h + 1) * dk] for h in hs]
    k = [k_ref[0, :, h * dk:(h + 1) * dk] for h in hs]
    v = [v_ref[0, :, h * dv:(h + 1) * dv] for h in hs]
    s = [s_ref[0, 0, h] for h in hs]
    gcc = [gc_col[:, h:h + 1] for h in hs]
    bc = [beta[:, h:h + 1] for h in hs]
    gt = [g_tot[:, h:h + 1] for h in hs]
    dec_incl = []
    for h in hs:
        diff = gcc[h] - gc_row[h:h + 1, :]
        dec_incl.append(jnp.where(incl, jnp.exp(jnp.where(incl, diff, 0.0)), 0.0))
    kk = [_dotb_nt(k[h], k[h]) for h in hs]
    qk = [_dotb_nt(q[h], k[h]) * dec_incl[h] for h in hs]
    m = [bc[h] * kk[h] * jnp.where(strict, dec_incl[h], 0.0) for h in hs]
    t = _unit_tri_inverses(m, ii, jj, lo, hi)
    egc = [jnp.exp(gcc[h]) for h in hs]
    sol = [_dotb(t[h], jnp.concatenate([bc[h] * v[h], (bc[h] * egc[h]) * k[h]], axis=1)) for h in hs]
    qs = [_dotb(q[h] * egc[h], s[h]) for h in hs]
    u = [sol[h][:, :dv] - _dotb(sol[h][:, dv:], s[h]) for h in hs]
    o = [qs[h] + _dotb(qk[h], u[h]) for h in hs]
    s_new = [jnp.exp(gt[h]) * s[h] + _dotb_tn(k[h] * jnp.exp(gt[h] - gcc[h]), u[h]) for h in hs]
    for h in hs:
        s_ref[0, 0, h] = s_new[h]
        o_ref[0, 0, :, h * dv:(h + 1) * dv] = o[h]


def delta_scan(q, k, v, beta, g, s0):
    bsz, length, _ = q.shape
    _, _, heads, dk, dv = s0.shape
    c = CHUNK
    n = length // c
    gt = jnp.swapaxes(g.reshape(bsz, 2, n, c, heads), 3, 4)

    def chunk_of(d, i):
        return i + d * (n - 1 - 2 * i)

    tok = lambda width: pl.BlockSpec((1, c, width), lambda b, d, i: (b, chunk_of(d, i), 0))
    par = pl.BlockSpec((1, 1, c, heads), lambda b, d, i: (b, d, chunk_of(d, i), 0))
    st = pl.BlockSpec((1, 1, heads, dk, dv), lambda b, d, i: (b, d, 0, 0, 0))
    return pl.pallas_call(
        functools.partial(_delta_body, heads=heads, dk=dk, dv=dv),
        grid=(bsz, 2, n),
        in_specs=[tok(heads * dk), tok(heads * dk), tok(heads * dv), par, par,
                  pl.BlockSpec((1, 1, 1, heads, c), lambda b, d, i: (b, d, chunk_of(d, i), 0, 0)),
                  st],
        out_specs=[pl.BlockSpec((1, 1, c, heads * dv), lambda b, d, i: (b, d, chunk_of(d, i), 0)), st],
        out_shape=[jax.ShapeDtypeStruct((bsz, 2, length, heads * dv), F32),
                   jax.ShapeDtypeStruct(s0.shape, F32)],
        compiler_params=_cparams("parallel", "parallel", "arbitrary"),
    )(q, k, v, beta, g, gt, s0)


def _rwkv_body(r_ref, v_ref, kk_ref, lw_ref, kd_ref, ag_ref, s0_ref, y_ref, s_ref, *, heads, hn):
    d = pl.program_id(1)
    n = pl.program_id(2)
    rev = d == 1
    c = r_ref.shape[1]

    @pl.when(n == 0)
    def _():
        s_ref[...] = s0_ref[...]

    ii, jj, lo, hi = _order_masks(c, rev)
    incl = lo >= hi
    strict = lo > hi
    m_incl = incl.astype(F32)

    logw = lw_ref[0, 0]
    lw = _dot_hi(m_incl, logw)
    lw_tot = jnp.sum(logw, axis=0, keepdims=True)
    r = r_ref[0]
    v = v_ref[0]
    kk = kk_ref[0]
    kd = kd_ref[0, 0]
    bb = kk * ag_ref[0, 0]
    e_out = jnp.exp(-lw)
    e_rem = jnp.exp(lw_tot - lw)
    a_t = -kk * jnp.exp(lw - logw)
    r_t = r * jnp.exp(lw)
    b_t = bb * e_out
    k_t = kd * e_out
    b_p = bb * e_rem
    k_p = kd * e_rem
    p_c = jnp.exp(lw_tot)

    hs = range(heads)
    sl = [slice(h * hn, (h + 1) * hn) for h in hs]
    s = [s_ref[0, 0, h] for h in hs]
    lhs = [jnp.concatenate([a_t[:, sl[h]], r_t[:, sl[h]]], axis=0) for h in hs]
    rhs = [jnp.concatenate([b_t[:, sl[h]], k_t[:, sl[h]]], axis=0) for h in hs]
    gram = [_dotb_nt(lhs[h], rhs[h]) for h in hs]
    ls = [_dotb_nt(lhs[h], s[h]) for h in hs]
    t = _unit_tri_inverses([jnp.where(strict, -gram[h][:c, :c], 0.0) for h in hs], ii, jj, lo, hi)
    vh = [v[:, sl[h]] for h in hs]
    pre = [ls[h][:c] + _dotb(jnp.where(strict, gram[h][:c, c:], 0.0), vh[h]) for h in hs]
    uv = [jnp.concatenate([_dotb(t[h], pre[h]), vh[h]], axis=0) for h in hs]
    i2 = lax.broadcasted_iota(jnp.int32, (c, 2 * c), 0)
    j2 = lax.broadcasted_iota(jnp.int32, (c, 2 * c), 1) & (c - 1)
    incl2 = jnp.where(rev, j2, i2) >= jnp.where(rev, i2, j2)
    y = [ls[h][c:] + _dotb(jnp.where(incl2, gram[h][c:], 0.0), uv[h]) for h in hs]
    s_new = [s[h] * p_c[:, sl[h]]
             + _dotb_tn(uv[h], jnp.concatenate([b_p[:, sl[h]], k_p[:, sl[h]]], axis=0)) for h in hs]
    for h in hs:
        s_ref[0, 0, h] = s_new[h]
        y_ref[0, 0, :, sl[h]] = y[h]


def rwkv_scan(r, v, kk, logw, kdir, ag, s0):
    bsz, length, width = r.shape
    _, _, heads, hn, _ = s0.shape
    c = CHUNK
    n = length // c

    def chunk_of(d, i):
        return i + d * (n - 1 - 2 * i)

    tok = pl.BlockSpec((1, c, width), lambda b, d, i: (b, chunk_of(d, i), 0))
    par = pl.BlockSpec((1, 1, c, width), lambda b, d, i: (b, d, chunk_of(d, i), 0))
    st = pl.BlockSpec((1, 1, heads, hn, hn), lambda b, d, i: (b, d, 0, 0, 0))
    return pl.pallas_call(
        functools.partial(_rwkv_body, heads=heads, hn=hn),
        grid=(bsz, 2, n),
        in_specs=[tok, tok, tok, par, par, par, st],
        out_specs=[par, st],
        out_shape=[jax.ShapeDtypeStruct((bsz, 2, length, width), F32),
                   jax.ShapeDtypeStruct(s0.shape, F32)],
        compiler_params=_cparams("parallel", "parallel", "arbitrary"),
    )(r, v, kk, logw, kdir, ag, s0)


LRU_C = 8.0


def _scan_pass(a, b, shift, rows, rev):
    length = a.shape[0]
    if rev:
        a_s = pltpu.roll(a, length - shift, 0)
        b_s = pltpu.roll(b, length - shift, 0)
        valid = rows < length - shift
    else:
        a_s = pltpu.roll(a, shift, 0)
        b_s = pltpu.roll(b, shift, 0)
        valid = rows >= shift
    a_s = jnp.where(valid, a_s, 1.0)
    b_s = jnp.where(valid, b_s, 0.0)
    return a * a_s, a * b_s + b


def _lru_body(x_ref, gbr_ref, wg_ref, bg_ref, lam_ref, s0_ref, o_ref, s_ref):
    x = x_ref[0]
    length = x.shape[0]
    rows = lax.broadcasted_iota(jnp.int32, x.shape, 0)
    xb = x.astype(BF16)
    y = jnp.zeros_like(x)
    for d in range(2):
        rev = d == 1
        r_g = jax.nn.sigmoid(jnp.dot(xb, wg_ref[d, 0, 0].astype(BF16), preferred_element_type=F32)
                             + bg_ref[d, 0:1, :])
        i_g = jax.nn.sigmoid(jnp.dot(xb, wg_ref[d, 1, 0].astype(BF16), preferred_element_type=F32)
                             + bg_ref[d, 1:2, :])
        lam = lam_ref[d:d + 1, :]
        log_a = -LRU_C * r_g * jax.nn.softplus(-lam)
        a = jnp.exp(log_a)
        b = jnp.sqrt(-jnp.tanh(log_a) * (a * a + 1.0)) * (i_g * x)
        first = (rows == (length - 1 if rev else 0))
        b = jnp.where(first, b + a * s0_ref[0, d:d + 1, :], b)
        shift = 1
        while shift < length:
            a, b = _scan_pass(a, b, shift, rows, rev)
            shift *= 2
        y = y + b
        s_ref[0, d:d + 1, :] = b[0:1, :] if rev else b[length - 1:length, :]
    o_ref[0] = (jax.nn.gelu(gbr_ref[0]) * y).astype(o_ref.dtype)


def lru_mix(xc, gate_br, w_gate, b_gate, lam, s0):
    bsz, length, width = xc.shape
    nb, bs = w_gate.shape[2], w_gate.shape[3]
    tok = pl.BlockSpec((1, length, bs), lambda b, j: (b, 0, j))
    st = pl.BlockSpec((1, 2, bs), lambda b, j: (b, 0, j))
    return pl.pallas_call(
        _lru_body,
        grid=(bsz, nb),
        in_specs=[tok, tok,
                  pl.BlockSpec((2, 2, 1, bs, bs), lambda b, j: (0, 0, j, 0, 0)),
                  pl.BlockSpec((2, 2, bs), lambda b, j: (0, 0, j)),
                  pl.BlockSpec((2, bs), lambda b, j: (0, j)),
                  st],
        out_specs=[tok, st],
        out_shape=[jax.ShapeDtypeStruct((bsz, length, width), BF16),
                   jax.ShapeDtypeStruct((bsz, 2, width), F32)],
        compiler_params=_cparams("parallel", "parallel"),
    )(xc, gate_br, w_gate, b_gate, lam, s0)


def _dwconv(x, w_ref, cols, grid_w, image_rows):
    tokens = x.shape[0]
    w = lambda i, j: w_ref[3 * i + j:3 * i + j + 1, :]
    x_l = pltpu.roll(jnp.where(cols != grid_w - 1, x, 0.0), 1, 0)
    x_r = pltpu.roll(jnp.where(cols != 0, x, 0.0), tokens - 1, 0)
    out = w(1, 0) * x_l + w(1, 1) * x + w(1, 2) * x_r
    if image_rows > 1:
        assert tokens == grid_w * image_rows
        up = w(0, 0) * x_l + w(0, 1) * x + w(0, 2) * x_r
        dn = w(2, 0) * x_l + w(2, 1) * x + w(2, 2) * x_r
        zero = jnp.zeros((grid_w, x.shape[1]), x.dtype)
        out = out + jnp.concatenate([zero, up[:tokens - grid_w]], axis=0)
        out = out + jnp.concatenate([dn[grid_w:], zero], axis=0)
    return out


def _ffn_up_body(h_ref, wa_ref, wb_ref, ca_ref, cb_ref, o_ref, *, grid_w, image_rows):
    h = h_ref[...]
    cols = lax.broadcasted_iota(jnp.int32, o_ref.shape, 0) & (grid_w - 1)
    ua = jnp.dot(h, wa_ref[...], preferred_element_type=F32)
    ub = jnp.dot(h, wb_ref[...], preferred_element_type=F32)
    ca = _dwconv(ua, ca_ref, cols, grid_w, image_rows)
    cb = _dwconv(ub, cb_ref, cols, grid_w, image_rows)
    o_ref[...] = (jax.nn.silu(ca) * cb).astype(o_ref.dtype)


def ffn_up_conv(h, w_up, w_conv, grid_w, image_rows, tb=1024, tc=512):
    t, dm = h.shape
    f = w_up.shape[1] // 2
    nf = f // tc
    assert t % tb == 0 and f % tc == 0 and tb % (grid_w * image_rows) == 0
    assert image_rows == 1 or tb == grid_w * image_rows
    return pl.pallas_call(
        functools.partial(_ffn_up_body, grid_w=grid_w, image_rows=image_rows),
        grid=(t // tb, nf),
        in_specs=[pl.BlockSpec((tb, dm), lambda i, j: (i, 0)),
                  pl.BlockSpec((dm, tc), lambda i, j: (0, j)),
                  pl.BlockSpec((dm, tc), lambda i, j: (0, j + nf)),
                  pl.BlockSpec((9, tc), lambda i, j: (0, j)),
                  pl.BlockSpec((9, tc), lambda i, j: (0, j + nf))],
        out_specs=pl.BlockSpec((tb, tc), lambda i, j: (i, j)),
        out_shape=jax.ShapeDtypeStruct((t, f), BF16),
        compiler_params=_cparams("parallel", "arbitrary"),
    )(h, w_up, w_up, w_conv, w_conv)


def _rms_norm(x, g):
    return x * lax.rsqrt(jnp.mean(x * x, axis=-1, keepdims=True) + NORM_EPS) * g


def _l2norm(x):
    return x * lax.rsqrt(jnp.sum(x * x, axis=-1, keepdims=True) + 1e-6)


def _short_conv1d(x, w, left):
    k, length = w.shape[0], x.shape[1]
    xp = jnp.pad(x, ((0, 0), (left, k - 1 - left), (0, 0)))
    out = xp[:, 0:length] * w[0]
    for j in range(1, k):
        out = out + xp[:, j:j + length] * w[j]
    return out


def _token_shift(x, mu_prev, mu_next):
    prev = jnp.pad(x, ((0, 0), (1, 0), (0, 0)))[:, :-1]
    nxt = jnp.pad(x, ((0, 0), (0, 1), (0, 0)))[:, 1:]
    return x + mu_prev * (prev - x) + mu_next * (nxt - x)


def _dir_major(x):
    return jnp.swapaxes(x, 1, 2)


def _even_mixer(h, w_in, dn_conv, dn_a_log, dn_dt_bias, dn_norm_g,
                rw_mu, rw_w0, rw_w2, rw_a0, rw_a2, rw_g2, rw_k_k, rw_k_a, rw_r_k, rw_ln_g, rw_ln_b,
                s_dn, s_rw):
    bsz, length, dm = h.shape
    dn_heads, dn_dk, dn_dv = s_dn.shape[2:]
    rw_heads, rw_n = s_rw.shape[2:4]
    dn_qk, dn_vw, rw_c = dn_heads * dn_dk, dn_heads * dn_dv, rw_heads * rw_n
    r_decay, r_a, r_gate = rw_w2.shape[1], rw_a2.shape[1], rw_g2.shape[0]
    p = matmul(h.reshape(bsz * length, dm), _pad_cols(w_in, 512))[:, :w_in.shape[1]]
    p = p.reshape(bsz, length, -1)
    ev_a = 2 * dn_qk + 2 * dn_vw + 4 * dn_heads
    p_a, p_b = p[..., :ev_a], p[..., ev_a:]
    qkv_raw = p_a[..., :2 * dn_qk + dn_vw]
    zg = p_a[..., 2 * dn_qk + dn_vw:2 * dn_qk + 2 * dn_vw]
    b_raw = p_a[..., 2 * dn_qk + 2 * dn_vw:2 * dn_qk + 2 * dn_vw + 2 * dn_heads]
    a_raw = p_a[..., 2 * dn_qk + 2 * dn_vw + 2 * dn_heads:]
    qkv = jax.nn.silu(_short_conv1d(qkv_raw, dn_conv, 2))
    q, k, v = qkv[..., :dn_qk], qkv[..., dn_qk:2 * dn_qk], qkv[..., 2 * dn_qk:]
    q = (_l2norm(q.reshape(bsz, length, dn_heads, dn_dk)) * (dn_dk ** -0.5)).reshape(bsz, length, dn_qk)
    k = _l2norm(k.reshape(bsz, length, dn_heads, dn_dk)).reshape(bsz, length, dn_qk)
    beta = jax.nn.sigmoid(b_raw).reshape(bsz, length, 2, dn_heads)
    g = -jnp.exp(dn_a_log) * jax.nn.softplus(a_raw.reshape(bsz, length, 2, dn_heads) + dn_dt_bias)
    o, s_dn_fin = delta_scan(q, k, v, _dir_major(beta), _dir_major(g), s_dn)
    o = (o[:, 0] + o[:, 1]).reshape(bsz, length, dn_heads, dn_dv)
    o = _rms_norm(o, dn_norm_g) * jax.nn.silu(zg).reshape(bsz, length, dn_heads, dn_dv)
    o_a = o.reshape(bsz, length, dn_vw)
    xs = _token_shift(p_b, rw_mu[0], rw_mu[1])
    r, kr, vr = xs[..., :rw_c], xs[..., rw_c:2 * rw_c], xs[..., 2 * rw_c:3 * rw_c]
    off = 3 * rw_c
    wd = xs[..., off:off + 2 * r_decay].reshape(bsz * length, 2, r_decay)
    ad = xs[..., off + 2 * r_decay:off + 2 * r_decay + 2 * r_a].reshape(bsz * length, 2, r_a)
    gd = xs[..., off + 2 * r_decay + 2 * r_a:].reshape(bsz * length, r_gate)
    w_pre = jnp.stack([matmul(jnp.tanh(wd[:, i]), rw_w2[i]) for i in range(2)], axis=1) + rw_w0
    log_decay = -jnp.exp(-jax.nn.softplus(-w_pre) - 0.5)
    a_g = jax.nn.sigmoid(jnp.stack([matmul(ad[:, i], rw_a2[i]) for i in range(2)], axis=1) + rw_a0)
    gate = matmul(jax.nn.sigmoid(gd), rw_g2).reshape(bsz, length, rw_c)
    kk = _l2norm((kr * rw_k_k).reshape(bsz, length, rw_heads, rw_n)).reshape(bsz, length, rw_c)
    a_g = a_g.reshape(bsz, length, 2, rw_c)
    log_decay = log_decay.reshape(bsz, length, 2, rw_c)
    k_dir = kr[:, :, None, :] * (1.0 + (a_g - 1.0) * rw_k_a)
    y, s_rw_fin = rwkv_scan(r, vr, kk, _dir_major(log_decay), _dir_major(k_dir), _dir_major(a_g), s_rw)
    y = (y[:, 0] + y[:, 1]).reshape(bsz, length, rw_heads, rw_n)
    mean = jnp.mean(y, axis=-1, keepdims=True)
    var = jnp.mean(jnp.square(y - mean), axis=-1, keepdims=True)
    yn = ((y - mean) * lax.rsqrt(var + rw_n * 1e-5)).reshape(bsz, length, rw_c) * rw_ln_g + rw_ln_b
    rk = jnp.sum(r[:, :, None, :] * k_dir, axis=2).reshape(bsz, length, rw_heads, rw_n)
    bonus = (jnp.sum(rk * rw_r_k, axis=-1, keepdims=True)
             * vr.reshape(bsz, length, rw_heads, rw_n)).reshape(bsz, length, rw_c)
    o_b = (yn + bonus) * gate
    cat = jnp.concatenate([o_a, o_b], axis=-1).reshape(bsz * length, dn_vw + rw_c)
    return cat.astype(BF16), s_dn_fin, s_rw_fin


def _pad_cols(w, mult):
    pad = (-w.shape[1]) % mult
    return jnp.pad(w, ((0, 0), (0, pad))) if pad else w


def _odd_mixer(h, w_in, conv_w, conv_b, w_gate, b_gate, lam, s_lru):
    bsz, length, dm = h.shape
    width = s_lru.shape[-1]
    p = matmul(h.reshape(bsz * length, dm), w_in).reshape(bsz, length, 2 * width)
    gate_br, x_br = p[..., :width], p[..., width:]
    xc = _short_conv1d(x_br, conv_w, 2) + conv_b
    a, s_fin = lru_mix(xc, gate_br, w_gate, b_gate, lam, s_lru)
    return a.reshape(bsz * length, width), s_fin


def _trunk(x, mod, rows, s_dn, s_rw, s_lru, prm):
    depth = mod.shape[0]
    bsz, length, dm = x.shape
    fin_dn, fin_rw, fin_lru = [], [], []
    x = x.reshape(bsz * length, dm)
    h = norm_mod(x, prm["norm_g"][0, 0], mod[0], 0)
    for l in range(depth):
        g = prm["norm_g"][l]
        h = h.reshape(bsz, length, dm)
        i = l // 2
        if l % 2 == 0:
            mix, sd, sr = _even_mixer(
                h, prm["ev_w_in"][i], prm["dn_conv"][i], prm["dn_a_log"][i],
                prm["dn_dt_bias"][i], prm["dn_norm_g"][i], prm["rw_mu"][i], prm["rw_w0"][i],
                prm["rw_w2"][i], prm["rw_a0"][i], prm["rw_a2"][i], prm["rw_g2"][i], prm["rw_k_k"][i],
                prm["rw_k_a"][i], prm["rw_r_k"][i], prm["rw_ln_g"][i], prm["rw_ln_b"][i],
                s_dn[:, i], s_rw[:, i])
            fin_dn.append(sd)
            fin_rw.append(sr)
            w_out = prm["ev_w_out"][i]
        else:
            mix, sl = _odd_mixer(h, prm["od_w_in"][i], prm["lru_conv"][i],
                                 prm["lru_conv_b"][i], prm["lru_w_gate"][i], prm["lru_b_gate"][i],
                                 prm["lru_lambda"][i], s_lru[:, i])
            fin_lru.append(sl)
            w_out = prm["od_w_out"][i]
        x, h = matmul_residual(mix, w_out.astype(BF16), x, g[1], mod[l], 2, next_norm=(g[2], mod[l], 3))
        mid = ffn_up_conv(h, prm["ffn_w_up"][l].astype(BF16), prm["ffn_conv"][l].reshape(9, -1),
                          length // rows, rows)
        nxt = (prm["norm_g"][l + 1, 0], mod[l + 1], 0) if l + 1 < depth else None
        x, h = matmul_residual(mid, prm["ffn_w_down"][l].astype(BF16), x, g[3], mod[l], 5, next_norm=nxt)
    return (x.reshape(bsz, length, dm), jnp.stack(fin_dn, axis=1), jnp.stack(fin_rw, axis=1),
            jnp.stack(fin_lru, axis=1))


def kernel(x_prompt, x_sample, state_dn, state_rwkv, state_lru, c, c_ctx, w_mod, b_mod, norm_g, ffn_w_up, ffn_conv, ffn_w_down, ev_w_in, ev_w_out, dn_conv, dn_a_log, dn_dt_bias, dn_norm_g, rw_mu, rw_w0, rw_w2, rw_a0, rw_a2, rw_g2, rw_k_k, rw_k_a, rw_r_k, rw_ln_g, rw_ln_b, od_w_in, od_w_out, lru_conv, lru_conv_b, lru_w_gate, lru_b_gate, lru_lambda):
    prm = dict(w_mod=w_mod, b_mod=b_mod, norm_g=norm_g, ffn_w_up=ffn_w_up, ffn_conv=ffn_conv,
               ffn_w_down=ffn_w_down, ev_w_in=ev_w_in, ev_w_out=ev_w_out, dn_conv=dn_conv,
               dn_a_log=dn_a_log, dn_dt_bias=dn_dt_bias, dn_norm_g=dn_norm_g, rw_mu=rw_mu, rw_w0=rw_w0,
               rw_w2=rw_w2, rw_a0=rw_a0, rw_a2=rw_a2, rw_g2=rw_g2, rw_k_k=rw_k_k, rw_k_a=rw_k_a,
               rw_r_k=rw_r_k, rw_ln_g=rw_ln_g, rw_ln_b=rw_ln_b, od_w_in=od_w_in, od_w_out=od_w_out,
               lru_conv=lru_conv, lru_conv_b=lru_conv_b, lru_w_gate=lru_w_gate, lru_b_gate=lru_b_gate,
               lru_lambda=lru_lambda)
    bp = x_prompt.shape[0]
    grid_w = 64
    depth, dm = w_mod.shape[0], w_mod.shape[1]
    cond = jnp.concatenate([c_ctx[None, :], c], axis=0)
    mod = modulation_all(jnp.pad(cond, ((0, (-cond.shape[0]) % 16), (0, 0))), w_mod, b_mod)
    mod = mod.reshape(depth, -1, 6, 1, dm)
    y_prompt, new_dn, new_rw, new_lru = _trunk(
        x_prompt, mod[:, :1], 1,
        jnp.zeros((bp,) + state_dn.shape[1:], F32),
        jnp.zeros((bp,) + state_rwkv.shape[1:], F32),
        jnp.zeros((bp,) + state_lru.shape[1:], F32), prm)
    rows = x_sample.shape[1] // grid_w
    y_sample, _, _, _ = _trunk(x_sample, mod[:, 1:1 + c.shape[0]], rows, state_dn, state_rwkv, state_lru,
                               prm)
    return (y_prompt, y_sample, new_dn, new_rw, new_lru)
```

```python
import functools
import math

import jax
import jax.numpy as jnp
from jax import lax
from jax.experimental import pallas as pl
from jax.experimental.pallas import tpu as pltpu

F32 = jnp.float32
BF16 = jnp.bfloat16
HIGHEST = lax.Precision.HIGHEST

NORM_EPS = 1e-6
CHUNK = 64
VMEM_LIMIT_BYTES = 48 * 1024 * 1024


def _cparams(*sem):
    return pltpu.CompilerParams(dimension_semantics=sem, vmem_limit_bytes=VMEM_LIMIT_BYTES)


def _dotb(a, b):
    return jnp.dot(a.astype(BF16), b.astype(BF16), preferred_element_type=F32)


def _dotb_nt(a, b):
    return lax.dot_general(a.astype(BF16), b.astype(BF16), (((1,), (1,)), ((), ())),
                           preferred_element_type=F32)


def _dotb_tn(a, b):
    return lax.dot_general(a.astype(BF16), b.astype(BF16), (((0,), (0,)), ((), ())),
                           preferred_element_type=F32)


def _dot_hi(a, b):
    return jnp.dot(a, b, preferred_element_type=F32, precision=HIGHEST)


def _mm_body(a_ref, b_ref, o_ref):
    o_ref[...] = jnp.dot(a_ref[...], b_ref[...], preferred_element_type=F32).astype(o_ref.dtype)


LANES = 128


def _tile(n, target):
    if n <= target:
        return n
    best = None
    for t in range(LANES, target + 1, LANES):
        if n % t == 0:
            best = t
    assert best is not None, n
    return best


def matmul(a, b, out_dtype=F32, tm=1024, tn=1536):
    m, k = a.shape
    n = b.shape[1]
    tm, tn = _tile(m, tm), _tile(n, tn)
    return pl.pallas_call(
        _mm_body,
        grid=(m // tm, n // tn),
        in_specs=[pl.BlockSpec((tm, k), lambda i, j: (i, 0)),
                  pl.BlockSpec((k, tn), lambda i, j: (0, j))],
        out_specs=pl.BlockSpec((tm, tn), lambda i, j: (i, j)),
        out_shape=jax.ShapeDtypeStruct((m, n), out_dtype),
        compiler_params=_cparams("parallel", "arbitrary"),
    )(a.astype(BF16), b.astype(BF16))


def _mod_body(c_ref, w_ref, b_ref, o_ref):
    o_ref[0] = jnp.dot(c_ref[...], w_ref[0].astype(BF16), preferred_element_type=F32) + b_ref[0]


def modulation_all(cvec, w_mod, b_mod, tn=1024):
    nc, dm = cvec.shape
    depth, _, n = w_mod.shape
    return pl.pallas_call(
        _mod_body,
        grid=(depth, n // tn),
        in_specs=[pl.BlockSpec((nc, dm), lambda l, j: (0, 0)),
                  pl.BlockSpec((1, dm, tn), lambda l, j: (l, 0, j)),
                  pl.BlockSpec((1, 1, tn), lambda l, j: (l, 0, j))],
        out_specs=pl.BlockSpec((1, nc, tn), lambda l, j: (l, 0, j)),
        out_shape=jax.ShapeDtypeStruct((depth, nc, n), F32),
        compiler_params=_cparams("parallel", "parallel"),
    )(jax.nn.silu(cvec).astype(BF16), w_mod, b_mod.reshape(depth, 1, n))


def _norm_mod(x, g, scale, shift):
    return x * lax.rsqrt(jnp.mean(x * x, axis=-1, keepdims=True) + NORM_EPS) * g * (1.0 + scale) + shift


def _mod_spec(which, blocks_per_cond, dm):
    return pl.BlockSpec((1, 1, 1, dm), lambda i, *_: (i // blocks_per_cond, which, 0, 0))


def _norm_mod_body(x_ref, g_ref, sc_ref, sh_ref, h_ref):
    h_ref[...] = _norm_mod(x_ref[...], g_ref[...], sc_ref[0, 0], sh_ref[0, 0]).astype(h_ref.dtype)


def norm_mod(x, g, mod, which_shift, tm=512):
    t, dm = x.shape
    bpc = t // mod.shape[0] // tm
    row = pl.BlockSpec((tm, dm), lambda i: (i, 0))
    return pl.pallas_call(
        _norm_mod_body,
        grid=(t // tm,),
        in_specs=[row, pl.BlockSpec((1, dm), lambda i: (0, 0)),
                  _mod_spec(which_shift + 1, bpc, dm), _mod_spec(which_shift, bpc, dm)],
        out_specs=row,
        out_shape=jax.ShapeDtypeStruct((t, dm), BF16),
        compiler_params=_cparams("parallel"),
    )(x, g.reshape(1, dm), mod, mod)


def _mm_res_body(a_ref, w_ref, x_ref, g_ref, gate_ref, *rest, nk, emit_h):
    if emit_h:
        gn_ref, sc_ref, sh_ref, xo_ref, h_ref, acc_ref = rest
    else:
        xo_ref, acc_ref = rest
    k = pl.program_id(1)

    @pl.when(k == 0)
    def _():
        acc_ref[...] = jnp.zeros_like(acc_ref)

    acc_ref[...] += jnp.dot(a_ref[...], w_ref[...], preferred_element_type=F32)

    @pl.when(k == nk - 1)
    def _():
        y = acc_ref[...]
        y = y * lax.rsqrt(jnp.mean(y * y, axis=-1, keepdims=True) + NORM_EPS) * g_ref[...]
        xn = x_ref[...] + gate_ref[0, 0] * y
        xo_ref[...] = xn
        if emit_h:
            h_ref[...] = _norm_mod(xn, gn_ref[...], sc_ref[0, 0], sh_ref[0, 0]).astype(h_ref.dtype)


def matmul_residual(a, w, x, g_out, mod, which_gate, next_norm=None, tm=512, tk=1408):
    t, kdim = a.shape
    dm = w.shape[1]
    tk = _tile(kdim, tk)
    nk = kdim // tk
    bpc = t // mod.shape[0] // tm
    row = pl.BlockSpec((tm, dm), lambda i, k: (i, 0))
    vec = pl.BlockSpec((1, dm), lambda i, k: (0, 0))
    in_specs = [pl.BlockSpec((tm, tk), lambda i, k: (i, k)),
                pl.BlockSpec((tk, dm), lambda i, k: (k, 0)),
                row, vec, _mod_spec(which_gate, bpc, dm)]
    args = [a, w, x, g_out.reshape(1, dm), mod]
    out_specs = [row]
    out_shape = [jax.ShapeDtypeStruct((t, dm), F32)]
    if next_norm is not None:
        g_next, mod_next, which_shift = next_norm
        in_specs += [vec, _mod_spec(which_shift + 1, bpc, dm), _mod_spec(which_shift, bpc, dm)]
        args += [g_next.reshape(1, dm), mod_next, mod_next]
        out_specs.append(row)
        out_shape.append(jax.ShapeDtypeStruct((t, dm), BF16))
    out = pl.pallas_call(
        functools.partial(_mm_res_body, nk=nk, emit_h=next_norm is not None),
        grid=(t // tm, nk),
        in_specs=in_specs,
        out_specs=out_specs,
        out_shape=out_shape,
        scratch_shapes=[pltpu.VMEM((tm, dm), F32)],
        compiler_params=_cparams("parallel", "arbitrary"),
    )(*args)
    return out if next_norm is not None else (out[0], None)


def _order_masks(c, rev):
    ii = lax.broadcasted_iota(jnp.int32, (c, c), 0)
    jj = lax.broadcasted_iota(jnp.int32, (c, c), 1)
    lo = jnp.where(rev, jj, ii)
    hi = jnp.where(rev, ii, jj)
    return ii, jj, lo, hi


def _unit_tri_inverses(ms, ii, jj, lo, hi):
    c = ms[0].shape[0]
    eye = (ii == jj).astype(F32)

    def pair_mask(shift):
        same = (lo >> (shift + 1)) == (hi >> (shift + 1))
        return same & (((lo >> shift) & 1) == 1) & (((hi >> shift) & 1) == 0)

    mask = pair_mask(0)
    ts = [eye - jnp.where(mask, m, 0.0) for m in ms]
    shift = 1
    while (1 << shift) < c:
        mask = pair_mask(shift)
        tb = [_dotb(t, jnp.where(mask, m, 0.0)) for t, m in zip(ts, ms)]
        ts = [t - _dotb(x, t) for x, t in zip(tb, ts)]
        shift += 1
    return ts


def _delta_body(q_ref, k_ref, v_ref, beta_ref, g_ref, gt_ref, s0_ref, o_ref, s_ref, *, heads, dk, dv):
    d = pl.program_id(1)
    n = pl.program_id(2)
    rev = d == 1
    c = q_ref.shape[1]

    @pl.when(n == 0)
    def _():
        s_ref[...] = s0_ref[...]

    ii, jj, lo, hi = _order_masks(c, rev)
    incl = lo >= hi
    strict = lo > hi
    m_incl = incl.astype(F32)

    g = g_ref[0, 0]
    beta = beta_ref[0, 0]
    gc_col = _dot_hi(m_incl, g)
    gc_row = lax.dot_general(gt_ref[0, 0, 0], m_incl, (((1,), (1,)), ((), ())),
                             preferred_element_type=F32, precision=HIGHEST)
    g_tot = jnp.sum(g, axis=0, keepdims=True)

    hs = range(heads)
    q = [q_ref[0, :, h * dk:(h + 1) * dk] for h in hs]
    k = [k_ref[0, :, h * dk:(h + 1) * dk] for h in hs]
    v = [v_ref[0, :, h * dv:(h + 1) * dv] for h in hs]
    s = [s_ref[0, 0, h] for h in hs]
    gcc = [gc_col[:, h:h + 1] for h in hs]
    bc = [beta[:, h:h + 1] for h in hs]
    gt = [g_tot[:, h:h + 1] for h in hs]
    dec_incl = []
    for h in hs:
        diff = gcc[h] - gc_row[h:h + 1, :]
        dec_incl.append(jnp.where(incl, jnp.exp(jnp.where(incl, diff, 0.0)), 0.0))
    kk = [_dotb_nt(k[h], k[h]) for h in hs]
    qk = [_dotb_nt(q[h], k[h]) * dec_incl[h] for h in hs]
    m = [bc[h] * kk[h] * jnp.where(strict, dec_incl[h], 0.0) for h in hs]
    t = _unit_tri_inverses(m, ii, jj, lo, hi)
    egc = [jnp.exp(gcc[h]) for h in hs]
    sol = [_dotb(t[h], jnp.concatenate([bc[h] * v[h], (bc[h] * egc[h]) * k[h]], axis=1)) for h in hs]
    qs = [_dotb(q[h] * egc[h], s[h]) for h in hs]
    u = [sol[h][:, :dv] - _dotb(sol[h][:, dv:], s[h]) for h in hs]
    o = [qs[h] + _dotb(qk[h], u[h]) for h in hs]
    s_new = [jnp.exp(gt[h]) * s[h] + _dotb_tn(k[h] * jnp.exp(gt[h] - gcc[h]), u[h]) for h in hs]
    for h in hs:
        s_ref[0, 0, h] = s_new[h]
        o_ref[0, 0, :, h * dv:(h + 1) * dv] = o[h]


def delta_scan(qkv, beta, g, s0):
    bsz, length, _ = qkv.shape
    _, _, heads, dk, dv = s0.shape
    assert dk == dv
    c = CHUNK
    n = length // c
    gt = jnp.swapaxes(g.reshape(bsz, 2, n, c, heads), 3, 4)

    def chunk_of(d, i):
        return i + d * (n - 1 - 2 * i)

    tok = lambda part: pl.BlockSpec((1, c, heads * dk), lambda b, d, i: (b, chunk_of(d, i), part))
    par = pl.BlockSpec((1, 1, c, heads), lambda b, d, i: (b, d, chunk_of(d, i), 0))
    st = pl.BlockSpec((1, 1, heads, dk, dv), lambda b, d, i: (b, d, 0, 0, 0))
    return pl.pallas_call(
        functools.partial(_delta_body, heads=heads, dk=dk, dv=dv),
        grid=(bsz, 2, n),
        in_specs=[tok(0), tok(1), tok(2), par, par,
                  pl.BlockSpec((1, 1, 1, heads, c), lambda b, d, i: (b, d, chunk_of(d, i), 0, 0)),
                  st],
        out_specs=[pl.BlockSpec((1, 1, c, heads * dv), lambda b, d, i: (b, d, chunk_of(d, i), 0)), st],
        out_shape=[jax.ShapeDtypeStruct((bsz, 2, length, heads * dv), F32),
                   jax.ShapeDtypeStruct(s0.shape, F32)],
        compiler_params=_cparams("parallel", "parallel", "arbitrary"),
    )(qkv, qkv, qkv, beta, g, gt, s0)


def _shift_rows(x, s, rows):
    length = x.shape[0]
    y = pltpu.roll(x, s % length, 0)
    return jnp.where(rows >= s if s > 0 else rows < length + s, y, 0.0)


def _short_conv(x, w_ref, rows):
    return (w_ref[0:1, :] * _shift_rows(x, 2, rows) + w_ref[1:2, :] * _shift_rows(x, 1, rows)
            + w_ref[2:3, :] * x + w_ref[3:4, :] * _shift_rows(x, -1, rows))


def _token_shift(x, mu_ref, rows):
    return (x + mu_ref[0:1, :] * (_shift_rows(x, 1, rows) - x)
            + mu_ref[1:2, :] * (_shift_rows(x, -1, rows) - x))


def _group_ones(width, group):
    i = lax.broadcasted_iota(jnp.int32, (width, width), 0) // group
    j = lax.broadcasted_iota(jnp.int32, (width, width), 1) // group
    return (i == j).astype(BF16)


def _group_sum(x, ones):
    hi = x.astype(BF16)
    lo = (x - hi.astype(F32)).astype(BF16)
    return (jnp.dot(hi, ones, preferred_element_type=F32) + jnp.dot(lo, ones, preferred_element_type=F32))


def _delta_prep_body(p_ref, w_ref, o_ref, *, dk, q_blocks):
    j = pl.program_id(1)
    rows = lax.broadcasted_iota(jnp.int32, p_ref.shape[1:], 0)
    y = jax.nn.silu(_short_conv(p_ref[0], w_ref, rows))
    norm_w = jnp.where(j < q_blocks, dk ** -0.5, jnp.where(j < 2 * q_blocks, 1.0, 0.0))
    plain_w = jnp.where(j < 2 * q_blocks, 0.0, 1.0)
    for h in range(y.shape[1] // dk):
        yh = y[:, h * dk:(h + 1) * dk]
        inv = lax.rsqrt(jnp.sum(yh * yh, axis=-1, keepdims=True) + 1e-6)
        o_ref[0, :, h * dk:(h + 1) * dk] = yh * (inv * norm_w + plain_w)


def delta_prep(p, conv_w, heads, dk, tc=512):
    bsz, length, _ = p.shape
    width = heads * dk
    spec = pl.BlockSpec((1, length, tc), lambda b, j: (b, 0, j))
    return pl.pallas_call(
        functools.partial(_delta_prep_body, dk=dk, q_blocks=width // tc),
        grid=(bsz, 3 * width // tc),
        in_specs=[spec, pl.BlockSpec((4, tc), lambda b, j: (0, j))],
        out_specs=spec,
        out_shape=jax.ShapeDtypeStruct((bsz, length, 3 * width), F32),
        compiler_params=_cparams("parallel", "parallel"),
    )(p, conv_w)


def _rwkv_prep_body(r_ref, k_ref, v_ref, sm_ref, mur_ref, muk_ref, muv_ref, mus_ref,
                    w0_ref, w2_ref, a0_ref, a2_ref, g2_ref, kkw_ref, kaw_ref,
                    ro_ref, vo_ref, kk_ref, lw_ref, kd_ref, ag_ref, gate_ref, *, hn, r_decay, r_a, r_gate):
    rows = lax.broadcasted_iota(jnp.int32, r_ref.shape[1:], 0)
    rows_s = lax.broadcasted_iota(jnp.int32, sm_ref.shape[1:], 0)
    ro_ref[0] = _token_shift(r_ref[0], mur_ref, rows)
    vo_ref[0] = _token_shift(v_ref[0], muv_ref, rows)
    kr = _token_shift(k_ref[0], muk_ref, rows)
    sm = _token_shift(sm_ref[0], mus_ref, rows_s)
    wd = jnp.tanh(sm[:, :2 * r_decay])
    ad = sm[:, 2 * r_decay:2 * r_decay + 2 * r_a]
    gd = jax.nn.sigmoid(sm[:, 2 * r_decay + 2 * r_a:2 * r_decay + 2 * r_a + r_gate])
    gate_ref[0] = _dotb(gd, g2_ref[...])
    kx = kr * kkw_ref[...]
    ones = _group_ones(kx.shape[1], hn)
    kk_ref[0] = kx * lax.rsqrt(_group_sum(kx * kx, ones) + 1e-6)
    for d in range(2):
        w_pre = w0_ref[d:d + 1, :] + _dotb(wd[:, d * r_decay:(d + 1) * r_decay], w2_ref[d])
        lw_ref[0, d] = -jnp.exp(-jax.nn.softplus(-w_pre) - 0.5)
        a_g = jax.nn.sigmoid(a0_ref[d:d + 1, :] + _dotb(ad[:, d * r_a:(d + 1) * r_a], a2_ref[d]))
        ag_ref[0, d] = a_g
        kd_ref[0, d] = kr * (1.0 + (a_g - 1.0) * kaw_ref[...])


def rwkv_prep(p, col0, small_col0, small_w, mu_rkv, mu_small, w0, w2, a0, a2, g2, k_k, k_a, hn, tc=256):
    bsz, length, _ = p.shape
    width = w0.shape[1]
    r_decay, r_a, r_gate = w2.shape[1], a2.shape[1], g2.shape[0]
    nt = width // tc
    assert col0 % tc == 0 and small_col0 % small_w == 0 and width % tc == 0

    def part(i):
        return pl.BlockSpec((1, length, tc), lambda b, j: (b, 0, col0 // tc + i * nt + j))

    def mu_part(i):
        return pl.BlockSpec((2, tc), lambda b, j: (0, i * nt + j))

    vec = pl.BlockSpec((1, tc), lambda b, j: (0, j))
    dvec = pl.BlockSpec((2, tc), lambda b, j: (0, j))
    tok = pl.BlockSpec((1, length, tc), lambda b, j: (b, 0, j))
    dtok = pl.BlockSpec((1, 2, length, tc), lambda b, j: (b, 0, 0, j))
    one = jax.ShapeDtypeStruct((bsz, length, width), F32)
    two = jax.ShapeDtypeStruct((bsz, 2, length, width), F32)
    return pl.pallas_call(
        functools.partial(_rwkv_prep_body, hn=hn, r_decay=r_decay, r_a=r_a, r_gate=r_gate),
        grid=(bsz, nt),
        in_specs=[part(0), part(1), part(2),
                  pl.BlockSpec((1, length, small_w), lambda b, j: (b, 0, small_col0 // small_w)),
                  mu_part(0), mu_part(1), mu_part(2),
                  pl.BlockSpec((2, small_w), lambda b, j: (0, 0)),
                  dvec, pl.BlockSpec((2, r_decay, tc), lambda b, j: (0, 0, j)),
                  dvec, pl.BlockSpec((2, r_a, tc), lambda b, j: (0, 0, j)),
                  pl.BlockSpec((r_gate, tc), lambda b, j: (0, j)), vec, vec],
        out_specs=[tok, tok, tok, dtok, dtok, dtok, tok],
        out_shape=[one, one, one, two, two, two, one],
        compiler_params=_cparams("parallel", "parallel"),
    )(p, p, p, p, mu_rkv, mu_rkv, mu_rkv, mu_small, w0, w2, a0, a2, g2,
      k_k.reshape(1, width), k_a.reshape(1, width))


def _even_post_body(of_ref, ob_ref, z_ref, dg_ref, yf_ref, yb_ref, r_ref, v_ref, kf_ref, kb_ref,
                    gate_ref, rk_ref, lg_ref, lb_ref, out_ref, *, dv, hn, gn_eps):
    o = of_ref[0, 0] + ob_ref[0, 0]
    wa = o.shape[1]
    z = z_ref[0]
    for h in range(wa // dv):
        sl = slice(h * dv, (h + 1) * dv)
        oh = o[:, sl]
        oh = oh * lax.rsqrt(jnp.mean(oh * oh, axis=-1, keepdims=True) + NORM_EPS) * dg_ref[...]
        out_ref[0, :, sl] = (oh * jax.nn.silu(z[:, sl])).astype(out_ref.dtype)
    y = yf_ref[0, 0] + yb_ref[0, 0]
    lane = 128
    ones = _group_ones(lane, hn)
    for t in range(y.shape[1] // lane):
        sl = slice(t * lane, (t + 1) * lane)
        yt = y[:, sl]
        cen = yt - _group_sum(yt, ones) * (1.0 / hn)
        var = _group_sum(cen * cen, ones) * (1.0 / hn)
        yn = cen * lax.rsqrt(var + gn_eps) * lg_ref[:, sl] + lb_ref[:, sl]
        rk = r_ref[0, :, sl] * (kf_ref[0, 0, :, sl] + kb_ref[0, 0, :, sl]) * rk_ref[:, sl]
        bonus = _group_sum(rk, ones) * v_ref[0, :, sl]
        out_ref[0, :, wa + t * lane:wa + (t + 1) * lane] = ((yn + bonus) * gate_ref[0, :, sl]).astype(
            out_ref.dtype)


def even_post(o, p, z_col, dn_norm_g, y, r, v, kdir, gate, r_k, ln_g, ln_b, hn, tl=256):
    bsz, _, length, wa = o.shape
    wb = y.shape[3]
    dv = dn_norm_g.shape[0]
    assert z_col % wa == 0
    dirs = lambda d, w: pl.BlockSpec((1, 1, tl, w), lambda b, i: (b, d, i, 0))
    tok = lambda w: pl.BlockSpec((1, tl, w), lambda b, i: (b, i, 0))
    vec = lambda w: pl.BlockSpec((1, w), lambda b, i: (0, 0))
    return pl.pallas_call(
        functools.partial(_even_post_body, dv=dv, hn=hn, gn_eps=hn * 1e-5),
        grid=(bsz, length // tl),
        in_specs=[dirs(0, wa), dirs(1, wa),
                  pl.BlockSpec((1, tl, wa), lambda b, i: (b, i, z_col // wa)), vec(dv),
                  dirs(0, wb), dirs(1, wb), tok(wb), tok(wb), dirs(0, wb), dirs(1, wb), tok(wb),
                  vec(wb), vec(wb), vec(wb)],
        out_specs=tok(wa + wb),
        out_shape=jax.ShapeDtypeStruct((bsz, length, wa + wb), BF16),
        compiler_params=_cparams("parallel", "parallel"),
    )(o, o, p, dn_norm_g.reshape(1, dv), y, y, r, v, kdir, kdir, gate,
      r_k.reshape(1, wb), ln_g.reshape(1, wb), ln_b.reshape(1, wb))


def _rwkv_body(r_ref, v_ref, kk_ref, lw_ref, kd_ref, ag_ref, s0_ref, y_ref, s_ref, *, heads, hn):
    d = pl.program_id(1)
    n = pl.program_id(2)
    rev = d == 1
    c = r_ref.shape[1]

    @pl.when(n == 0)
    def _():
        s_ref[...] = s0_ref[...]

    ii, jj, lo, hi = _order_masks(c, rev)
    incl = lo >= hi
    strict = lo > hi
    m_incl = incl.astype(F32)

    logw = lw_ref[0, 0]
    lw = _dot_hi(m_incl, logw)
    lw_tot = jnp.sum(logw, axis=0, keepdims=True)
    r = r_ref[0]
    v = v_ref[0]
    kk = kk_ref[0]
    kd = kd_ref[0, 0]
    bb = kk * ag_ref[0, 0]
    e_out = jnp.exp(-lw)
    e_rem = jnp.exp(lw_tot - lw)
    a_t = -kk * jnp.exp(lw - logw)
    r_t = r * jnp.exp(lw)
    b_t = bb * e_out
    k_t = kd * e_out
    b_p = bb * e_rem
    k_p = kd * e_rem
    p_c = jnp.exp(lw_tot)

    hs = range(heads)
    sl = [slice(h * hn, (h + 1) * hn) for h in hs]
    s = [s_ref[0, 0, h] for h in hs]
    lhs = [jnp.concatenate([a_t[:, sl[h]], r_t[:, sl[h]]], axis=0) for h in hs]
    rhs = [jnp.concatenate([b_t[:, sl[h]], k_t[:, sl[h]]], axis=0) for h in hs]
    gram = [_dotb_nt(lhs[h], rhs[h]) for h in hs]
    ls = [_dotb_nt(lhs[h], s[h]) for h in hs]
    t = _unit_tri_inverses([jnp.where(strict, -gram[h][:c, :c], 0.0) for h in hs], ii, jj, lo, hi)
    vh = [v[:, sl[h]] for h in hs]
    pre = [ls[h][:c] + _dotb(jnp.where(strict, gram[h][:c, c:], 0.0), vh[h]) for h in hs]
    uv = [jnp.concatenate([_dotb(t[h], pre[h]), vh[h]], axis=0) for h in hs]
    i2 = lax.broadcasted_iota(jnp.int32, (c, 2 * c), 0)
    j2 = lax.broadcasted_iota(jnp.int32, (c, 2 * c), 1) & (c - 1)
    incl2 = jnp.where(rev, j2, i2) >= jnp.where(rev, i2, j2)
    y = [ls[h][c:] + _dotb(jnp.where(incl2, gram[h][c:], 0.0), uv[h]) for h in hs]
    s_new = [s[h] * p_c[:, sl[h]]
             + _dotb_tn(uv[h], jnp.concatenate([b_p[:, sl[h]], k_p[:, sl[h]]], axis=0)) for h in hs]
    for h in hs:
        s_ref[0, 0, h] = s_new[h]
        y_ref[0, 0, :, sl[h]] = y[h]


def rwkv_scan(r, v, kk, logw, kdir, ag, s0):
    bsz, length, width = r.shape
    _, _, heads, hn, _ = s0.shape
    c = CHUNK
    n = length // c

    def chunk_of(d, i):
        return i + d * (n - 1 - 2 * i)

    tok = pl.BlockSpec((1, c, width), lambda b, d, i: (b, chunk_of(d, i), 0))
    par = pl.BlockSpec((1, 1, c, width), lambda b, d, i: (b, d, chunk_of(d, i), 0))
    st = pl.BlockSpec((1, 1, heads, hn, hn), lambda b, d, i: (b, d, 0, 0, 0))
    return pl.pallas_call(
        functools.partial(_rwkv_body, heads=heads, hn=hn),
        grid=(bsz, 2, n),
        in_specs=[tok, tok, tok, par, par, par, st],
        out_specs=[par, st],
        out_shape=[jax.ShapeDtypeStruct((bsz, 2, length, width), F32),
                   jax.ShapeDtypeStruct(s0.shape, F32)],
        compiler_params=_cparams("parallel", "parallel", "arbitrary"),
    )(r, v, kk, logw, kdir, ag, s0)


LRU_C = 8.0


def _scan_pass(a, b, shift, rows, rev):
    length = a.shape[0]
    if rev:
        a_s = pltpu.roll(a, length - shift, 0)
        b_s = pltpu.roll(b, length - shift, 0)
        valid = rows < length - shift
    else:
        a_s = pltpu.roll(a, shift, 0)
        b_s = pltpu.roll(b, shift, 0)
        valid = rows >= shift
    a_s = jnp.where(valid, a_s, 1.0)
    b_s = jnp.where(valid, b_s, 0.0)
    return a * a_s, a * b_s + b


def _lru_body(x_ref, gbr_ref, cw_ref, cb_ref, wg_ref, bg_ref, lam_ref, s0_ref, o_ref, s_ref):
    rows = lax.broadcasted_iota(jnp.int32, x_ref.shape[1:], 0)
    x = _short_conv(x_ref[0], cw_ref, rows) + cb_ref[...]
    length = x.shape[0]
    xb = x.astype(BF16)
    y = jnp.zeros_like(x)
    for d in range(2):
        rev = d == 1
        r_g = jax.nn.sigmoid(jnp.dot(xb, wg_ref[d, 0, 0].astype(BF16), preferred_element_type=F32)
                             + bg_ref[d, 0:1, :])
        i_g = jax.nn.sigmoid(jnp.dot(xb, wg_ref[d, 1, 0].astype(BF16), preferred_element_type=F32)
                             + bg_ref[d, 1:2, :])
        lam = lam_ref[d:d + 1, :]
        log_a = -LRU_C * r_g * jax.nn.softplus(-lam)
        a = jnp.exp(log_a)
        b = jnp.sqrt(-jnp.tanh(log_a) * (a * a + 1.0)) * (i_g * x)
        first = (rows == (length - 1 if rev else 0))
        b = jnp.where(first, b + a * s0_ref[0, d:d + 1, :], b)
        shift = 1
        while shift < length:
            a, b = _scan_pass(a, b, shift, rows, rev)
            shift *= 2
        y = y + b
        s_ref[0, d:d + 1, :] = b[0:1, :] if rev else b[length - 1:length, :]
    o_ref[0] = (jax.nn.gelu(gbr_ref[0]) * y).astype(o_ref.dtype)


def lru_mix(p, conv_w, conv_b, w_gate, b_gate, lam, s0):
    bsz, length, _ = p.shape
    nb, bs = w_gate.shape[2], w_gate.shape[3]
    width = nb * bs
    tok = pl.BlockSpec((1, length, bs), lambda b, j: (b, 0, j))
    st = pl.BlockSpec((1, 2, bs), lambda b, j: (b, 0, j))
    return pl.pallas_call(
        _lru_body,
        grid=(bsz, nb),
        in_specs=[pl.BlockSpec((1, length, bs), lambda b, j: (b, 0, nb + j)), tok,
                  pl.BlockSpec((4, bs), lambda b, j: (0, j)),
                  pl.BlockSpec((1, bs), lambda b, j: (0, j)),
                  pl.BlockSpec((2, 2, 1, bs, bs), lambda b, j: (0, 0, j, 0, 0)),
                  pl.BlockSpec((2, 2, bs), lambda b, j: (0, 0, j)),
                  pl.BlockSpec((2, bs), lambda b, j: (0, j)),
                  st],
        out_specs=[tok, st],
        out_shape=[jax.ShapeDtypeStruct((bsz, length, width), BF16),
                   jax.ShapeDtypeStruct((bsz, 2, width), F32)],
        compiler_params=_cparams("parallel", "parallel"),
    )(p, p, conv_w, conv_b.reshape(1, width), w_gate, b_gate, lam, s0)


def _dwconv(x, w_ref, cols, grid_w, image_rows):
    tokens = x.shape[0]
    w = lambda i, j: w_ref[3 * i + j:3 * i + j + 1, :]
    x_l = pltpu.roll(jnp.where(cols != grid_w - 1, x, 0.0), 1, 0)
    x_r = pltpu.roll(jnp.where(cols != 0, x, 0.0), tokens - 1, 0)
    out = w(1, 0) * x_l + w(1, 1) * x + w(1, 2) * x_r
    if image_rows > 1:
        assert tokens == grid_w * image_rows
        up = w(0, 0) * x_l + w(0, 1) * x + w(0, 2) * x_r
        dn = w(2, 0) * x_l + w(2, 1) * x + w(2, 2) * x_r
        zero = jnp.zeros((grid_w, x.shape[1]), x.dtype)
        out = out + jnp.concatenate([zero, up[:tokens - grid_w]], axis=0)
        out = out + jnp.concatenate([dn[grid_w:], zero], axis=0)
    return out


def _ffn_up_body(h_ref, wa_ref, wb_ref, ca_ref, cb_ref, o_ref, *, grid_w, image_rows):
    h = h_ref[...]
    cols = lax.broadcasted_iota(jnp.int32, o_ref.shape, 0) & (grid_w - 1)
    ua = jnp.dot(h, wa_ref[...], preferred_element_type=F32)
    ub = jnp.dot(h, wb_ref[...], preferred_element_type=F32)
    ca = _dwconv(ua, ca_ref, cols, grid_w, image_rows)
    cb = _dwconv(ub, cb_ref, cols, grid_w, image_rows)
    o_ref[...] = (jax.nn.silu(ca) * cb).astype(o_ref.dtype)


def ffn_up_conv(h, w_up, w_conv, grid_w, image_rows, tb=1024, tc=512):
    t, dm = h.shape
    f = w_up.shape[1] // 2
    nf = f // tc
    assert t % tb == 0 and f % tc == 0 and tb % (grid_w * image_rows) == 0
    assert image_rows == 1 or tb == grid_w * image_rows
    return pl.pallas_call(
        functools.partial(_ffn_up_body, grid_w=grid_w, image_rows=image_rows),
        grid=(t // tb, nf),
        in_specs=[pl.BlockSpec((tb, dm), lambda i, j: (i, 0)),
                  pl.BlockSpec((dm, tc), lambda i, j: (0, j)),
                  pl.BlockSpec((dm, tc), lambda i, j: (0, j + nf)),
                  pl.BlockSpec((9, tc), lambda i, j: (0, j)),
                  pl.BlockSpec((9, tc), lambda i, j: (0, j + nf))],
        out_specs=pl.BlockSpec((tb, tc), lambda i, j: (i, j)),
        out_shape=jax.ShapeDtypeStruct((t, f), BF16),
        compiler_params=_cparams("parallel", "arbitrary"),
    )(h, w_up, w_up, w_conv, w_conv)


def _reorder_in_proj(w_in, dn_w, heads2, rw_c, small_w):
    a_end = dn_w + 2 * heads2
    small = jnp.concatenate([w_in[:, a_end + 3 * rw_c:], w_in[:, dn_w:a_end]], axis=1)
    small = jnp.pad(small, ((0, 0), (0, small_w - small.shape[1])))
    return jnp.concatenate([w_in[:, :dn_w], w_in[:, a_end:a_end + 3 * rw_c], small], axis=1)


def _even_mixer(h, w_in, dn_conv, dn_a_log, dn_dt_bias, dn_norm_g,
                rw_mu, rw_w0, rw_w2, rw_a0, rw_a2, rw_g2, rw_k_k, rw_k_a, rw_r_k, rw_ln_g, rw_ln_b,
                s_dn, s_rw):
    bsz, length, dm = h.shape
    dn_heads, dn_dk, dn_dv = s_dn.shape[2:]
    rw_heads, rw_n = s_rw.shape[2:4]
    dn_qk, rw_c = dn_heads * dn_dk, rw_heads * rw_n
    dn_w = 3 * dn_qk + dn_heads * dn_dv
    n_small = w_in.shape[1] - dn_w - 4 * dn_heads - 3 * rw_c
    small_w = 512
    small0 = dn_w + 3 * rw_c
    w_r = _reorder_in_proj(w_in, dn_w, 2 * dn_heads, rw_c, small_w)
    p = matmul(h.reshape(bsz * length, dm), w_r).reshape(bsz, length, -1)
    b_raw = p[..., small0 + n_small:small0 + n_small + 2 * dn_heads]
    a_raw = p[..., small0 + n_small + 2 * dn_heads:small0 + n_small + 4 * dn_heads]
    beta = jax.nn.sigmoid(b_raw).reshape(bsz, length, 2, dn_heads)
    g = -jnp.exp(dn_a_log) * jax.nn.softplus(a_raw.reshape(bsz, length, 2, dn_heads) + dn_dt_bias)
    qkv = delta_prep(p, dn_conv, dn_heads, dn_dk)
    o, s_dn_fin = delta_scan(qkv, jnp.swapaxes(beta, 1, 2), jnp.swapaxes(g, 1, 2), s_dn)
    mu_small = jnp.pad(rw_mu[:, 3 * rw_c:], ((0, 0), (0, small_w - n_small)))
    r, vr, kk, logw, kdir, a_g, gate = rwkv_prep(
        p, dn_w, small0, small_w, rw_mu[:, :3 * rw_c], mu_small, rw_w0, rw_w2, rw_a0, rw_a2, rw_g2,
        rw_k_k, rw_k_a, rw_n)
    y, s_rw_fin = rwkv_scan(r, vr, kk, logw, kdir, a_g, s_rw)
    cat = even_post(o, p, 3 * dn_qk, dn_norm_g, y, r, vr, kdir, gate, rw_r_k.reshape(-1), rw_ln_g, rw_ln_b,
                    rw_n)
    return cat.reshape(bsz * length, -1), s_dn_fin, s_rw_fin


def _odd_mixer(h, w_in, conv_w, conv_b, w_gate, b_gate, lam, s_lru):
    bsz, length, dm = h.shape
    p = matmul(h.reshape(bsz * length, dm), w_in).reshape(bsz, length, -1)
    a, s_fin = lru_mix(p, conv_w, conv_b, w_gate, b_gate, lam, s_lru)
    return a.reshape(bsz * length, -1), s_fin


def _trunk(x, mod, rows, s_dn, s_rw, s_lru, prm):
    depth = mod.shape[0]
    bsz, length, dm = x.shape
    fin_dn, fin_rw, fin_lru = [], [], []
    x = x.reshape(bsz * length, dm)
    h = norm_mod(x, prm["norm_g"][0, 0], mod[0], 0)
    for l in range(depth):
        g = prm["norm_g"][l]
        h = h.reshape(bsz, length, dm)
        i = l // 2
        if l % 2 == 0:
            mix, sd, sr = _even_mixer(
                h, prm["ev_w_in"][i], prm["dn_conv"][i], prm["dn_a_log"][i],
                prm["dn_dt_bias"][i], prm["dn_norm_g"][i], prm["rw_mu"][i], prm["rw_w0"][i],
                prm["rw_w2"][i], prm["rw_a0"][i], prm["rw_a2"][i], prm["rw_g2"][i], prm["rw_k_k"][i],
                prm["rw_k_a"][i], prm["rw_r_k"][i], prm["rw_ln_g"][i], prm["rw_ln_b"][i],
                s_dn[:, i], s_rw[:, i])
            fin_dn.append(sd)
            fin_rw.append(sr)
            w_out = prm["ev_w_out"][i]
        else:
            mix, sl = _odd_mixer(h, prm["od_w_in"][i], prm["lru_conv"][i],
                                 prm["lru_conv_b"][i], prm["lru_w_gate"][i], prm["lru_b_gate"][i],
                                 prm["lru_lambda"][i], s_lru[:, i])
            fin_lru.append(sl)
            w_out = prm["od_w_out"][i]
        x, h = matmul_residual(mix, w_out.astype(BF16), x, g[1], mod[l], 2, next_norm=(g[2], mod[l], 3))
        mid = ffn_up_conv(h, prm["ffn_w_up"][l].astype(BF16), prm["ffn_conv"][l].reshape(9, -1),
                          length // rows, rows)
        nxt = (prm["norm_g"][l + 1, 0], mod[l + 1], 0) if l + 1 < depth else None
        x, h = matmul_residual(mid, prm["ffn_w_down"][l].astype(BF16), x, g[3], mod[l], 5, next_norm=nxt)
    return (x.reshape(bsz, length, dm), jnp.stack(fin_dn, axis=1), jnp.stack(fin_rw, axis=1),
            jnp.stack(fin_lru, axis=1))


def kernel(x_prompt, x_sample, state_dn, state_rwkv, state_lru, c, c_ctx, w_mod, b_mod, norm_g, ffn_w_up, ffn_conv, ffn_w_down, ev_w_in, ev_w_out, dn_conv, dn_a_log, dn_dt_bias, dn_norm_g, rw_mu, rw_w0, rw_w2, rw_a0, rw_a2, rw_g2, rw_k_k, rw_k_a, rw_r_k, rw_ln_g, rw_ln_b, od_w_in, od_w_out, lru_conv, lru_conv_b, lru_w_gate, lru_b_gate, lru_lambda):
    prm = dict(w_mod=w_mod, b_mod=b_mod, norm_g=norm_g, ffn_w_up=ffn_w_up, ffn_conv=ffn_conv,
               ffn_w_down=ffn_w_down, ev_w_in=ev_w_in, ev_w_out=ev_w_out, dn_conv=dn_conv,
               dn_a_log=dn_a_log, dn_dt_bias=dn_dt_bias, dn_norm_g=dn_norm_g, rw_mu=rw_mu, rw_w0=rw_w0,
               rw_w2=rw_w2, rw_a0=rw_a0, rw_a2=rw_a2, rw_g2=rw_g2, rw_k_k=rw_k_k, rw_k_a=rw_k_a,
               rw_r_k=rw_r_k, rw_ln_g=rw_ln_g, rw_ln_b=rw_ln_b, od_w_in=od_w_in, od_w_out=od_w_out,
               lru_conv=lru_conv, lru_conv_b=lru_conv_b, lru_w_gate=lru_w_gate, lru_b_gate=lru_b_gate,
               lru_lambda=lru_lambda)
    bp = x_prompt.shape[0]
    grid_w = 64
    depth, dm = w_mod.shape[0], w_mod.shape[1]
    cond = jnp.concatenate([c_ctx[None, :], c], axis=0)
    mod = modulation_all(jnp.pad(cond, ((0, (-cond.shape[0]) % 16), (0, 0))), w_mod, b_mod)
    mod = mod.reshape(depth, -1, 6, 1, dm)
    y_prompt, new_dn, new_rw, new_lru = _trunk(
        x_prompt, mod[:, :1], 1,
        jnp.zeros((bp,) + state_dn.shape[1:], F32),
        jnp.zeros((bp,) + state_rwkv.shape[1:], F32),
        jnp.zeros((bp,) + state_lru.shape[1:], F32), prm)
    rows = x_sample.shape[1] // grid_w
    y_sample, _, _, _ = _trunk(x_sample, mod[:, 1:1 + c.shape[0]], rows, state_dn, state_rwkv, state_lru,
                               prm)
    return (y_prompt, y_sample, new_dn, new_rw, new_lru)
```

```python
import functools
import math

import jax
import jax.numpy as jnp
from jax import lax
from jax.experimental import pallas as pl
from jax.experimental.pallas import tpu as pltpu

F32 = jnp.float32
BF16 = jnp.bfloat16
HIGHEST = lax.Precision.HIGHEST

NORM_EPS = 1e-6
CHUNK = 64
VMEM_LIMIT_BYTES = 48 * 1024 * 1024


def _cparams(*sem):
    return pltpu.CompilerParams(dimension_semantics=sem, vmem_limit_bytes=VMEM_LIMIT_BYTES)


def _dotb(a, b):
    return jnp.dot(a.astype(BF16), b.astype(BF16), preferred_element_type=F32)


def _dotb_nt(a, b):
    return lax.dot_general(a.astype(BF16), b.astype(BF16), (((1,), (1,)), ((), ())),
                           preferred_element_type=F32)


def _dotb_tn(a, b):
    return lax.dot_general(a.astype(BF16), b.astype(BF16), (((0,), (0,)), ((), ())),
                           preferred_element_type=F32)


def _dot_hi(a, b):
    return jnp.dot(a, b, preferred_element_type=F32, precision=HIGHEST)


def _mm_body(a_ref, b_ref, o_ref):
    o_ref[...] = jnp.dot(a_ref[...], b_ref[...], preferred_element_type=F32).astype(o_ref.dtype)


LANES = 128


def _tile(n, target):
    if n <= target:
        return n
    best = None
    for t in range(LANES, target + 1, LANES):
        if n % t == 0:
            best = t
    assert best is not None, n
    return best


def matmul(a, b, out_dtype=F32, tm=1024, tn=1536):
    m, k = a.shape
    n = b.shape[1]
    tm, tn = _tile(m, tm), _tile(n, tn)
    return pl.pallas_call(
        _mm_body,
        grid=(m // tm, n // tn),
        in_specs=[pl.BlockSpec((tm, k), lambda i, j: (i, 0)),
                  pl.BlockSpec((k, tn), lambda i, j: (0, j))],
        out_specs=pl.BlockSpec((tm, tn), lambda i, j: (i, j)),
        out_shape=jax.ShapeDtypeStruct((m, n), out_dtype),
        compiler_params=_cparams("parallel", "arbitrary"),
    )(a.astype(BF16), b.astype(BF16))


def _mod_body(c_ref, w_ref, b_ref, o_ref):
    o_ref[0] = jnp.dot(c_ref[...], w_ref[0].astype(BF16), preferred_element_type=F32) + b_ref[0]


def modulation_all(cvec, w_mod, b_mod, tn=1024):
    nc, dm = cvec.shape
    depth, _, n = w_mod.shape
    return pl.pallas_call(
        _mod_body,
        grid=(depth, n // tn),
        in_specs=[pl.BlockSpec((nc, dm), lambda l, j: (0, 0)),
                  pl.BlockSpec((1, dm, tn), lambda l, j: (l, 0, j)),
                  pl.BlockSpec((1, 1, tn), lambda l, j: (l, 0, j))],
        out_specs=pl.BlockSpec((1, nc, tn), lambda l, j: (l, 0, j)),
        out_shape=jax.ShapeDtypeStruct((depth, nc, n), F32),
        compiler_params=_cparams("parallel", "parallel"),
    )(jax.nn.silu(cvec).astype(BF16), w_mod, b_mod.reshape(depth, 1, n))


def _norm_mod(x, g, scale, shift):
    return x * lax.rsqrt(jnp.mean(x * x, axis=-1, keepdims=True) + NORM_EPS) * g * (1.0 + scale) + shift


def _mod_spec(which, blocks_per_cond, dm):
    return pl.BlockSpec((1, 1, 1, dm), lambda i, *_: (i // blocks_per_cond, which, 0, 0))


def _norm_mod_body(x_ref, g_ref, sc_ref, sh_ref, h_ref):
    h_ref[...] = _norm_mod(x_ref[...], g_ref[...], sc_ref[0, 0], sh_ref[0, 0]).astype(h_ref.dtype)


def norm_mod(x, g, mod, which_shift, tm=512):
    t, dm = x.shape
    bpc = t // mod.shape[0] // tm
    row = pl.BlockSpec((tm, dm), lambda i: (i, 0))
    return pl.pallas_call(
        _norm_mod_body,
        grid=(t // tm,),
        in_specs=[row, pl.BlockSpec((1, dm), lambda i: (0, 0)),
                  _mod_spec(which_shift + 1, bpc, dm), _mod_spec(which_shift, bpc, dm)],
        out_specs=row,
        out_shape=jax.ShapeDtypeStruct((t, dm), BF16),
        compiler_params=_cparams("parallel"),
    )(x, g.reshape(1, dm), mod, mod)


def _mm_res_body(a_ref, w_ref, x_ref, g_ref, gate_ref, *rest, nk, emit_h):
    rest = list(rest)
    acc_ref = rest.pop() if nk > 1 else None
    if emit_h:
        gn_ref, sc_ref, sh_ref, xo_ref, h_ref = rest
    else:
        (xo_ref,) = rest
    k = pl.program_id(1)

    if nk > 1:
        @pl.when(k == 0)
        def _():
            acc_ref[...] = jnp.zeros_like(acc_ref)

        acc_ref[...] += jnp.dot(a_ref[...], w_ref[...], preferred_element_type=F32)

    @pl.when(k == nk - 1)
    def _():
        y = acc_ref[...] if nk > 1 else jnp.dot(a_ref[...], w_ref[...], preferred_element_type=F32)
        y = y * lax.rsqrt(jnp.mean(y * y, axis=-1, keepdims=True) + NORM_EPS) * g_ref[...]
        xn = x_ref[...] + gate_ref[0, 0] * y
        xo_ref[...] = xn
        if emit_h:
            h_ref[...] = _norm_mod(xn, gn_ref[...], sc_ref[0, 0], sh_ref[0, 0]).astype(h_ref.dtype)


def matmul_residual(a, w, x, g_out, mod, which_gate, next_norm=None, tm=512, tk=2048):
    t, kdim = a.shape
    dm = w.shape[1]
    tk = _tile(kdim, tk)
    nk = kdim // tk
    bpc = t // mod.shape[0] // tm
    row = pl.BlockSpec((tm, dm), lambda i, k: (i, 0))
    vec = pl.BlockSpec((1, dm), lambda i, k: (0, 0))
    in_specs = [pl.BlockSpec((tm, tk), lambda i, k: (i, k)),
                pl.BlockSpec((tk, dm), lambda i, k: (k, 0)),
                row, vec, _mod_spec(which_gate, bpc, dm)]
    args = [a, w, x, g_out.reshape(1, dm), mod]
    out_specs = [row]
    out_shape = [jax.ShapeDtypeStruct((t, dm), F32)]
    if next_norm is not None:
        g_next, mod_next, which_shift = next_norm
        in_specs += [vec, _mod_spec(which_shift + 1, bpc, dm), _mod_spec(which_shift, bpc, dm)]
        args += [g_next.reshape(1, dm), mod_next, mod_next]
        out_specs.append(row)
        out_shape.append(jax.ShapeDtypeStruct((t, dm), BF16))
    out = pl.pallas_call(
        functools.partial(_mm_res_body, nk=nk, emit_h=next_norm is not None),
        grid=(t // tm, nk),
        in_specs=in_specs,
        out_specs=out_specs,
        out_shape=out_shape,
        scratch_shapes=[pltpu.VMEM((tm, dm), F32)] if nk > 1 else [],
        compiler_params=_cparams("parallel", "arbitrary"),
    )(*args)
    return out if next_norm is not None else (out[0], None)


def _order_masks(c, rev):
    ii = lax.broadcasted_iota(jnp.int32, (c, c), 0)
    jj = lax.broadcasted_iota(jnp.int32, (c, c), 1)
    lo = jnp.where(rev, jj, ii)
    hi = jnp.where(rev, ii, jj)
    return ii, jj, lo, hi


def _unit_tri_inverses(ms, ii, jj, lo, hi, which=None):
    c = ms[0].shape[0]
    eye = (ii == jj).astype(F32)
    if which is None:
        lo, hi, which = [lo], [hi], [0] * len(ms)

    def pair_masks(shift):
        out = []
        for l, h in zip(lo, hi):
            same = (l >> (shift + 1)) == (h >> (shift + 1))
            out.append(same & (((l >> shift) & 1) == 1) & (((h >> shift) & 1) == 0))
        return out

    masks = pair_masks(0)
    ts = [eye - jnp.where(masks[w], m, 0.0) for m, w in zip(ms, which)]
    shift = 1
    while (1 << shift) < c:
        masks = pair_masks(shift)
        tb = [_dotb(t, jnp.where(masks[w], m, 0.0)) for t, m, w in zip(ts, ms, which)]
        ts = [t - _dotb(x, t) for x, t in zip(tb, ts)]
        shift += 1
    return ts


def _delta_body(qf_ref, kf_ref, vf_ref, qb_ref, kb_ref, vb_ref, betaf_ref, betab_ref, gf_ref, gb_ref,
                gtf_ref, gtb_ref, s0_ref, of_ref, ob_ref, s_ref, *, heads, dk, dv):
    c = qf_ref.shape[1]

    @pl.when(pl.program_id(1) == 0)
    def _():
        s_ref[...] = s0_ref[...]

    ii, jj = _order_masks(c, False)[:2]
    los, his = [ii, jj], [jj, ii]
    units = [(d, h) for d in range(2) for h in range(heads)]
    q_refs, k_refs, v_refs = (qf_ref, qb_ref), (kf_ref, kb_ref), (vf_ref, vb_ref)
    gc_col, gc_row, g_tot, beta, incl, strict = [], [], [], [], [], []
    for d, (g_ref, gt_ref, b_ref) in enumerate(((gf_ref, gtf_ref, betaf_ref), (gb_ref, gtb_ref, betab_ref))):
        incl.append(los[d] >= his[d])
        strict.append(los[d] > his[d])
        m_incl = incl[d].astype(F32)
        g = g_ref[0, 0]
        gc_col.append(_dot_hi(m_incl, g))
        gc_row.append(lax.dot_general(gt_ref[0, 0, 0], m_incl, (((1,), (1,)), ((), ())),
                                      preferred_element_type=F32, precision=HIGHEST))
        g_tot.append(jnp.sum(g, axis=0, keepdims=True))
        beta.append(b_ref[0, 0])

    q = [q_refs[d][0, :, h * dk:(h + 1) * dk] for d, h in units]
    k = [k_refs[d][0, :, h * dk:(h + 1) * dk] for d, h in units]
    v = [v_refs[d][0, :, h * dv:(h + 1) * dv] for d, h in units]
    s = [s_ref[0, d, h] for d, h in units]
    gcc = [gc_col[d][:, h:h + 1] for d, h in units]
    bc = [beta[d][:, h:h + 1] for d, h in units]
    gt = [g_tot[d][:, h:h + 1] for d, h in units]
    us = range(len(units))
    dec_incl = []
    for i, (d, h) in enumerate(units):
        diff = gcc[i] - gc_row[d][h:h + 1, :]
        dec_incl.append(jnp.where(incl[d], jnp.exp(jnp.where(incl[d], diff, 0.0)), 0.0))
    kk = [_dotb_nt(k[i], k[i]) for i in us]
    qk = [_dotb_nt(q[i], k[i]) * dec_incl[i] for i in us]
    m = [bc[i] * kk[i] * jnp.where(strict[units[i][0]], dec_incl[i], 0.0) for i in us]
    t = _unit_tri_inverses(m, ii, jj, los, his, which=[d for d, _ in units])
    egc = [jnp.exp(gcc[i]) for i in us]
    sol = [_dotb(t[i], jnp.concatenate([bc[i] * v[i], (bc[i] * egc[i]) * k[i]], axis=1)) for i in us]
    qs = [_dotb(q[i] * egc[i], s[i]) for i in us]
    u = [sol[i][:, :dv] - _dotb(sol[i][:, dv:], s[i]) for i in us]
    o = [qs[i] + _dotb(qk[i], u[i]) for i in us]
    s_new = [jnp.exp(gt[i]) * s[i] + _dotb_tn(k[i] * jnp.exp(gt[i] - gcc[i]), u[i]) for i in us]
    o_refs = (of_ref, ob_ref)
    for i, (d, h) in enumerate(units):
        s_ref[0, d, h] = s_new[i]
        o_refs[d][0, 0, :, h * dv:(h + 1) * dv] = o[i]


def delta_scan(qkv, beta, g, s0):
    bsz, length, _ = qkv.shape
    _, _, heads, dk, dv = s0.shape
    assert dk == dv
    c = CHUNK
    n = length // c
    gt = jnp.swapaxes(g.reshape(bsz, 2, n, c, heads), 3, 4)

    def chunk_of(d, i):
        return i + d * (n - 1 - 2 * i)

    tok = lambda part, d: pl.BlockSpec((1, c, heads * dk), lambda b, i: (b, chunk_of(d, i), part))
    par = lambda d: pl.BlockSpec((1, 1, c, heads), lambda b, i: (b, d, chunk_of(d, i), 0))
    row = lambda d: pl.BlockSpec((1, 1, 1, heads, c), lambda b, i: (b, d, chunk_of(d, i), 0, 0))
    st = pl.BlockSpec((1, 2, heads, dk, dv), lambda b, i: (b, 0, 0, 0, 0))
    out = lambda d: pl.BlockSpec((1, 1, c, heads * dv), lambda b, i: (0, b, chunk_of(d, i), 0))
    return pl.pallas_call(
        functools.partial(_delta_body, heads=heads, dk=dk, dv=dv),
        grid=(bsz, n),
        in_specs=[tok(0, 0), tok(1, 0), tok(2, 0), tok(0, 1), tok(1, 1), tok(2, 1),
                  par(0), par(1), par(0), par(1), row(0), row(1), st],
        out_specs=[out(0), out(1), st],
        out_shape=[jax.ShapeDtypeStruct((1, bsz, length, heads * dv), F32),
                   jax.ShapeDtypeStruct((1, bsz, length, heads * dv), F32),
                   jax.ShapeDtypeStruct(s0.shape, F32)],
        compiler_params=_cparams("parallel", "arbitrary"),
    )(qkv, qkv, qkv, qkv, qkv, qkv, beta, beta, g, g, gt, gt, s0)


def _shift_rows(x, s, rows):
    length = x.shape[0]
    y = pltpu.roll(x, s % length, 0)
    return jnp.where(rows >= s if s > 0 else rows < length + s, y, 0.0)


def _short_conv(x, w_ref, rows):
    return (w_ref[0:1, :] * _shift_rows(x, 2, rows) + w_ref[1:2, :] * _shift_rows(x, 1, rows)
            + w_ref[2:3, :] * x + w_ref[3:4, :] * _shift_rows(x, -1, rows))


def _token_shift(x, mu_ref, rows):
    return (x + mu_ref[0:1, :] * (_shift_rows(x, 1, rows) - x)
            + mu_ref[1:2, :] * (_shift_rows(x, -1, rows) - x))


def _group_ones(width, group):
    i = lax.broadcasted_iota(jnp.int32, (width, width), 0) // group
    j = lax.broadcasted_iota(jnp.int32, (width, width), 1) // group
    return (i == j).astype(BF16)


def _group_sum(x, ones):
    hi = x.astype(BF16)
    lo = (x - hi.astype(F32)).astype(BF16)
    return (jnp.dot(hi, ones, preferred_element_type=F32) + jnp.dot(lo, ones, preferred_element_type=F32))


def _delta_prep_body(p_ref, w_ref, o_ref, *, dk, q_blocks):
    j = pl.program_id(1)
    rows = lax.broadcasted_iota(jnp.int32, p_ref.shape[1:], 0)
    y = jax.nn.silu(_short_conv(p_ref[0], w_ref, rows))
    norm_w = jnp.where(j < q_blocks, dk ** -0.5, jnp.where(j < 2 * q_blocks, 1.0, 0.0))
    plain_w = jnp.where(j < 2 * q_blocks, 0.0, 1.0)
    for h in range(y.shape[1] // dk):
        yh = y[:, h * dk:(h + 1) * dk]
        inv = lax.rsqrt(jnp.sum(yh * yh, axis=-1, keepdims=True) + 1e-6)
        o_ref[0, :, h * dk:(h + 1) * dk] = yh * (inv * norm_w + plain_w)


def delta_prep(p, conv_w, heads, dk, tc=512):
    bsz, length, _ = p.shape
    width = heads * dk
    spec = pl.BlockSpec((1, length, tc), lambda b, j: (b, 0, j))
    return pl.pallas_call(
        functools.partial(_delta_prep_body, dk=dk, q_blocks=width // tc),
        grid=(bsz, 3 * width // tc),
        in_specs=[spec, pl.BlockSpec((4, tc), lambda b, j: (0, j))],
        out_specs=spec,
        out_shape=jax.ShapeDtypeStruct((bsz, length, 3 * width), F32),
        compiler_params=_cparams("parallel", "parallel"),
    )(p, conv_w)


def _rwkv_prep_body(r_ref, k_ref, v_ref, sm_ref, mur_ref, muk_ref, muv_ref, mus_ref,
                    w0_ref, w2_ref, a0_ref, a2_ref, g2_ref, kkw_ref, kaw_ref,
                    ro_ref, vo_ref, kk_ref, lw_ref, kd_ref, ag_ref, gate_ref, *, hn, r_decay, r_a, r_gate):
    rows = lax.broadcasted_iota(jnp.int32, r_ref.shape[1:], 0)
    rows_s = lax.broadcasted_iota(jnp.int32, sm_ref.shape[1:], 0)
    ro_ref[0] = _token_shift(r_ref[0], mur_ref, rows)
    vo_ref[0] = _token_shift(v_ref[0], muv_ref, rows)
    kr = _token_shift(k_ref[0], muk_ref, rows)
    sm = _token_shift(sm_ref[0], mus_ref, rows_s)
    wd = jnp.tanh(sm[:, :2 * r_decay])
    ad = sm[:, 2 * r_decay:2 * r_decay + 2 * r_a]
    gd = jax.nn.sigmoid(sm[:, 2 * r_decay + 2 * r_a:2 * r_decay + 2 * r_a + r_gate])
    gate_ref[0] = _dotb(gd, g2_ref[...])
    kx = kr * kkw_ref[...]
    ones = _group_ones(kx.shape[1], hn)
    kk_ref[0] = kx * lax.rsqrt(_group_sum(kx * kx, ones) + 1e-6)
    for d in range(2):
        w_pre = w0_ref[d:d + 1, :] + _dotb(wd[:, d * r_decay:(d + 1) * r_decay], w2_ref[d])
        lw_ref[0, d] = -jnp.exp(-jax.nn.softplus(-w_pre) - 0.5)
        a_g = jax.nn.sigmoid(a0_ref[d:d + 1, :] + _dotb(ad[:, d * r_a:(d + 1) * r_a], a2_ref[d]))
        ag_ref[0, d] = a_g
        kd_ref[0, d] = kr * (1.0 + (a_g - 1.0) * kaw_ref[...])


def rwkv_prep(p, col0, small_col0, small_w, mu_rkv, mu_small, w0, w2, a0, a2, g2, k_k, k_a, hn, tc=256):
    bsz, length, _ = p.shape
    width = w0.shape[1]
    r_decay, r_a, r_gate = w2.shape[1], a2.shape[1], g2.shape[0]
    nt = width // tc
    assert col0 % tc == 0 and small_col0 % small_w == 0 and width % tc == 0

    def part(i):
        return pl.BlockSpec((1, length, tc), lambda b, j: (b, 0, col0 // tc + i * nt + j))

    def mu_part(i):
        return pl.BlockSpec((2, tc), lambda b, j: (0, i * nt + j))

    vec = pl.BlockSpec((1, tc), lambda b, j: (0, j))
    dvec = pl.BlockSpec((2, tc), lambda b, j: (0, j))
    tok = pl.BlockSpec((1, length, tc), lambda b, j: (b, 0, j))
    dtok = pl.BlockSpec((1, 2, length, tc), lambda b, j: (b, 0, 0, j))
    one = jax.ShapeDtypeStruct((bsz, length, width), F32)
    two = jax.ShapeDtypeStruct((bsz, 2, length, width), F32)
    return pl.pallas_call(
        functools.partial(_rwkv_prep_body, hn=hn, r_decay=r_decay, r_a=r_a, r_gate=r_gate),
        grid=(bsz, nt),
        in_specs=[part(0), part(1), part(2),
                  pl.BlockSpec((1, length, small_w), lambda b, j: (b, 0, small_col0 // small_w)),
                  mu_part(0), mu_part(1), mu_part(2),
                  pl.BlockSpec((2, small_w), lambda b, j: (0, 0)),
                  dvec, pl.BlockSpec((2, r_decay, tc), lambda b, j: (0, 0, j)),
                  dvec, pl.BlockSpec((2, r_a, tc), lambda b, j: (0, 0, j)),
                  pl.BlockSpec((r_gate, tc), lambda b, j: (0, j)), vec, vec],
        out_specs=[tok, tok, tok, dtok, dtok, dtok, tok],
        out_shape=[one, one, one, two, two, two, one],
        compiler_params=_cparams("parallel", "parallel"),
    )(p, p, p, p, mu_rkv, mu_rkv, mu_rkv, mu_small, w0, w2, a0, a2, g2,
      k_k.reshape(1, width), k_a.reshape(1, width))


def _even_post_body(of_ref, ob_ref, z_ref, dg_ref, yf_ref, yb_ref, r_ref, v_ref, kf_ref, kb_ref,
                    gate_ref, rk_ref, lg_ref, lb_ref, out_ref, *, dv, hn, gn_eps):
    o = of_ref[0, 0] + ob_ref[0, 0]
    wa = o.shape[1]
    z = z_ref[0]
    for h in range(wa // dv):
        sl = slice(h * dv, (h + 1) * dv)
        oh = o[:, sl]
        oh = oh * lax.rsqrt(jnp.mean(oh * oh, axis=-1, keepdims=True) + NORM_EPS) * dg_ref[...]
        out_ref[0, :, sl] = (oh * jax.nn.silu(z[:, sl])).astype(out_ref.dtype)
    y = yf_ref[0, 0] + yb_ref[0, 0]
    lane = 128
    ones = _group_ones(lane, hn)
    for t in range(y.shape[1] // lane):
        sl = slice(t * lane, (t + 1) * lane)
        yt = y[:, sl]
        cen = yt - _group_sum(yt, ones) * (1.0 / hn)
        var = _group_sum(cen * cen, ones) * (1.0 / hn)
        yn = cen * lax.rsqrt(var + gn_eps) * lg_ref[:, sl] + lb_ref[:, sl]
        rk = r_ref[0, :, sl] * (kf_ref[0, 0, :, sl] + kb_ref[0, 0, :, sl]) * rk_ref[:, sl]
        bonus = _group_sum(rk, ones) * v_ref[0, :, sl]
        out_ref[0, :, wa + t * lane:wa + (t + 1) * lane] = ((yn + bonus) * gate_ref[0, :, sl]).astype(
            out_ref.dtype)


def even_post(o_f, o_b, p, z_col, dn_norm_g, y_f, y_b, r, v, kdir, gate, r_k, ln_g, ln_b, hn, tl=256):
    _, bsz, length, wa = o_f.shape
    lead = lambda w: pl.BlockSpec((1, 1, tl, w), lambda b, i: (0, b, i, 0))
    wb = y_f.shape[3]
    dv = dn_norm_g.shape[0]
    assert z_col % wa == 0
    dirs = lambda d, w: pl.BlockSpec((1, 1, tl, w), lambda b, i: (b, d, i, 0))
    tok = lambda w: pl.BlockSpec((1, tl, w), lambda b, i: (b, i, 0))
    vec = lambda w: pl.BlockSpec((1, w), lambda b, i: (0, 0))
    return pl.pallas_call(
        functools.partial(_even_post_body, dv=dv, hn=hn, gn_eps=hn * 1e-5),
        grid=(bsz, length // tl),
        in_specs=[lead(wa), lead(wa),
                  pl.BlockSpec((1, tl, wa), lambda b, i: (b, i, z_col // wa)), vec(dv),
                  lead(wb), lead(wb), tok(wb), tok(wb), dirs(0, wb), dirs(1, wb), tok(wb),
                  vec(wb), vec(wb), vec(wb)],
        out_specs=tok(wa + wb),
        out_shape=jax.ShapeDtypeStruct((bsz, length, wa + wb), BF16),
        compiler_params=_cparams("parallel", "parallel"),
    )(o_f, o_b, p, dn_norm_g.reshape(1, dv), y_f, y_b, r, v, kdir, kdir, gate,
      r_k.reshape(1, wb), ln_g.reshape(1, wb), ln_b.reshape(1, wb))


def _rwkv_body(rf_ref, vf_ref, kkf_ref, rb_ref, vb_ref, kkb_ref, lwf_ref, lwb_ref, kdf_ref, kdb_ref,
               agf_ref, agb_ref, s0_ref, yf_ref, yb_ref, s_ref, *, heads, hn):
    c = rf_ref.shape[1]

    @pl.when(pl.program_id(1) == 0)
    def _():
        s_ref[...] = s0_ref[...]

    ii, jj = _order_masks(c, False)[:2]
    los, his = [ii, jj], [jj, ii]
    i2 = lax.broadcasted_iota(jnp.int32, (c, 2 * c), 0)
    j2 = lax.broadcasted_iota(jnp.int32, (c, 2 * c), 1) & (c - 1)
    incl2 = [i2 >= j2, j2 >= i2]
    strict = [ii > jj, jj > ii]
    a_t, r_t, b_t, k_t, b_p, k_p, p_c, v = [], [], [], [], [], [], [], []
    dir_refs = ((rf_ref, vf_ref, kkf_ref, lwf_ref, kdf_ref, agf_ref),
                (rb_ref, vb_ref, kkb_ref, lwb_ref, kdb_ref, agb_ref))
    for d, (r_ref, v_ref, kk_ref, lw_ref, kd_ref, ag_ref) in enumerate(dir_refs):
        logw = lw_ref[0, 0]
        lw = _dot_hi((los[d] >= his[d]).astype(F32), logw)
        lw_tot = jnp.sum(logw, axis=0, keepdims=True)
        kk = kk_ref[0]
        kd = kd_ref[0, 0]
        bb = kk * ag_ref[0, 0]
        e_out = jnp.exp(-lw)
        e_rem = jnp.exp(lw_tot - lw)
        a_t.append(-kk * jnp.exp(lw - logw))
        r_t.append(r_ref[0] * jnp.exp(lw))
        b_t.append(bb * e_out)
        k_t.append(kd * e_out)
        b_p.append(bb * e_rem)
        k_p.append(kd * e_rem)
        p_c.append(jnp.exp(lw_tot))
        v.append(v_ref[0])

    units = [(d, h) for d in range(2) for h in range(heads)]
    us = range(len(units))
    sl = [slice(h * hn, (h + 1) * hn) for _, h in units]
    dr = [d for d, _ in units]
    s = [s_ref[0, d, h] for d, h in units]
    lhs = [jnp.concatenate([a_t[dr[i]][:, sl[i]], r_t[dr[i]][:, sl[i]]], axis=0) for i in us]
    rhs = [jnp.concatenate([b_t[dr[i]][:, sl[i]], k_t[dr[i]][:, sl[i]]], axis=0) for i in us]
    gram = [_dotb_nt(lhs[i], rhs[i]) for i in us]
    ls = [_dotb_nt(lhs[i], s[i]) for i in us]
    t = _unit_tri_inverses([jnp.where(strict[dr[i]], -gram[i][:c, :c], 0.0) for i in us],
                           ii, jj, los, his, which=dr)
    vh = [v[dr[i]][:, sl[i]] for i in us]
    pre = [ls[i][:c] + _dotb(jnp.where(strict[dr[i]], gram[i][:c, c:], 0.0), vh[i]) for i in us]
    uv = [jnp.concatenate([_dotb(t[i], pre[i]), vh[i]], axis=0) for i in us]
    y = [ls[i][c:] + _dotb(jnp.where(incl2[dr[i]], gram[i][c:], 0.0), uv[i]) for i in us]
    s_new = [s[i] * p_c[dr[i]][:, sl[i]]
             + _dotb_tn(uv[i], jnp.concatenate([b_p[dr[i]][:, sl[i]], k_p[dr[i]][:, sl[i]]], axis=0))
             for i in us]
    y_refs = (yf_ref, yb_ref)
    for i, (d, h) in enumerate(units):
        s_ref[0, d, h] = s_new[i]
        y_refs[d][0, 0, :, sl[i]] = y[i]


def rwkv_scan(r, v, kk, logw, kdir, ag, s0):
    bsz, length, width = r.shape
    _, _, heads, hn, _ = s0.shape
    c = CHUNK
    n = length // c

    def chunk_of(d, i):
        return i + d * (n - 1 - 2 * i)

    tok = lambda d: pl.BlockSpec((1, c, width), lambda b, i: (b, chunk_of(d, i), 0))
    par = lambda d: pl.BlockSpec((1, 1, c, width), lambda b, i: (b, d, chunk_of(d, i), 0))
    out = lambda d: pl.BlockSpec((1, 1, c, width), lambda b, i: (0, b, chunk_of(d, i), 0))
    st = pl.BlockSpec((1, 2, heads, hn, hn), lambda b, i: (b, 0, 0, 0, 0))
    one_dir = jax.ShapeDtypeStruct((1, bsz, length, width), F32)
    return pl.pallas_call(
        functools.partial(_rwkv_body, heads=heads, hn=hn),
        grid=(bsz, n),
        in_specs=[tok(0), tok(0), tok(0), tok(1), tok(1), tok(1),
                  par(0), par(1), par(0), par(1), par(0), par(1), st],
        out_specs=[out(0), out(1), st],
        out_shape=[one_dir, one_dir, jax.ShapeDtypeStruct(s0.shape, F32)],
        compiler_params=_cparams("parallel", "arbitrary"),
    )(r, v, kk, r, v, kk, logw, logw, kdir, kdir, ag, ag, s0)


LRU_C = 8.0


def _scan_pass(a, b, shift, rows, rev):
    length = a.shape[0]
    if rev:
        a_s = pltpu.roll(a, length - shift, 0)
        b_s = pltpu.roll(b, length - shift, 0)
        valid = rows < length - shift
    else:
        a_s = pltpu.roll(a, shift, 0)
        b_s = pltpu.roll(b, shift, 0)
        valid = rows >= shift
    a_s = jnp.where(valid, a_s, 1.0)
    b_s = jnp.where(valid, b_s, 0.0)
    return a * a_s, a * b_s + b


def _lru_body(x_ref, gbr_ref, cw_ref, cb_ref, wg_ref, bg_ref, lam_ref, s0_ref, o_ref, s_ref):
    rows = lax.broadcasted_iota(jnp.int32, x_ref.shape[1:], 0)
    x = _short_conv(x_ref[0], cw_ref, rows) + cb_ref[...]
    length = x.shape[0]
    xb = x.astype(BF16)
    y = jnp.zeros_like(x)
    for d in range(2):
        rev = d == 1
        r_g = jax.nn.sigmoid(jnp.dot(xb, wg_ref[d, 0, 0].astype(BF16), preferred_element_type=F32)
                             + bg_ref[d, 0:1, :])
        i_g = jax.nn.sigmoid(jnp.dot(xb, wg_ref[d, 1, 0].astype(BF16), preferred_element_type=F32)
                             + bg_ref[d, 1:2, :])
        lam = lam_ref[d:d + 1, :]
        log_a = -LRU_C * r_g * jax.nn.softplus(-lam)
        a = jnp.exp(log_a)
        b = jnp.sqrt(-jnp.tanh(log_a) * (a * a + 1.0)) * (i_g * x)
        first = (rows == (length - 1 if rev else 0))
        b = jnp.where(first, b + a * s0_ref[0, d:d + 1, :], b)
        shift = 1
        while shift < length:
            a, b = _scan_pass(a, b, shift, rows, rev)
            shift *= 2
        y = y + b
        s_ref[0, d:d + 1, :] = b[0:1, :] if rev else b[length - 1:length, :]
    o_ref[0] = (jax.nn.gelu(gbr_ref[0]) * y).astype(o_ref.dtype)


def lru_mix(p, conv_w, conv_b, w_gate, b_gate, lam, s0):
    bsz, length, _ = p.shape
    nb, bs = w_gate.shape[2], w_gate.shape[3]
    width = nb * bs
    tok = pl.BlockSpec((1, length, bs), lambda b, j: (b, 0, j))
    st = pl.BlockSpec((1, 2, bs), lambda b, j: (b, 0, j))
    return pl.pallas_call(
        _lru_body,
        grid=(bsz, nb),
        in_specs=[pl.BlockSpec((1, length, bs), lambda b, j: (b, 0, nb + j)), tok,
                  pl.BlockSpec((4, bs), lambda b, j: (0, j)),
                  pl.BlockSpec((1, bs), lambda b, j: (0, j)),
                  pl.BlockSpec((2, 2, 1, bs, bs), lambda b, j: (0, 0, j, 0, 0)),
                  pl.BlockSpec((2, 2, bs), lambda b, j: (0, 0, j)),
                  pl.BlockSpec((2, bs), lambda b, j: (0, j)),
                  st],
        out_specs=[tok, st],
        out_shape=[jax.ShapeDtypeStruct((bsz, length, width), BF16),
                   jax.ShapeDtypeStruct((bsz, 2, width), F32)],
        compiler_params=_cparams("parallel", "parallel"),
    )(p, p, conv_w, conv_b.reshape(1, width), w_gate, b_gate, lam, s0)


def _dwconv(x, w9, cols, grid_w, image_rows):
    tokens = x.shape[0]
    w = lambda i, j: w9[3 * i + j:3 * i + j + 1, :]
    x_l = pltpu.roll(jnp.where(cols != grid_w - 1, x, 0.0), 1, 0)
    x_r = pltpu.roll(jnp.where(cols != 0, x, 0.0), tokens - 1, 0)
    out = w(1, 0) * x_l + w(1, 1) * x + w(1, 2) * x_r
    if image_rows > 1:
        assert tokens == grid_w * image_rows
        up = w(0, 0) * x_l + w(0, 1) * x + w(0, 2) * x_r
        dn = w(2, 0) * x_l + w(2, 1) * x + w(2, 2) * x_r
        zero = jnp.zeros((grid_w, x.shape[1]), x.dtype)
        out = out + jnp.concatenate([zero, up[:tokens - grid_w]], axis=0)
        out = out + jnp.concatenate([dn[grid_w:], zero], axis=0)
    return out


def _ffn_up_body(h_ref, wa_ref, wb_ref, ca_ref, cb_ref, o_ref, *, grid_w, image_rows):
    h = h_ref[...]
    cols = lax.broadcasted_iota(jnp.int32, o_ref.shape, 0) & (grid_w - 1)
    ua = jnp.dot(h, wa_ref[...], preferred_element_type=F32)
    ub = jnp.dot(h, wb_ref[...], preferred_element_type=F32)
    ca = _dwconv(ua, ca_ref[...], cols, grid_w, image_rows)
    cb = _dwconv(ub, cb_ref[...], cols, grid_w, image_rows)
    o_ref[...] = (jax.nn.silu(ca) * cb).astype(o_ref.dtype)


def ffn_up_conv(h, w_up, w_conv, grid_w, image_rows, tb=1024, tc=512):
    t, dm = h.shape
    f = w_up.shape[1] // 2
    nf = f // tc
    assert t % tb == 0 and f % tc == 0 and tb % (grid_w * image_rows) == 0
    assert image_rows == 1 or tb == grid_w * image_rows
    return pl.pallas_call(
        functools.partial(_ffn_up_body, grid_w=grid_w, image_rows=image_rows),
        grid=(t // tb, nf),
        in_specs=[pl.BlockSpec((tb, dm), lambda i, j: (i, 0)),
                  pl.BlockSpec((dm, tc), lambda i, j: (0, j)),
                  pl.BlockSpec((dm, tc), lambda i, j: (0, j + nf)),
                  pl.BlockSpec((9, tc), lambda i, j: (0, j)),
                  pl.BlockSpec((9, tc), lambda i, j: (0, j + nf))],
        out_specs=pl.BlockSpec((tb, tc), lambda i, j: (i, j)),
        out_shape=jax.ShapeDtypeStruct((t, f), BF16),
        compiler_params=_cparams("parallel", "arbitrary"),
    )(h, w_up, w_up, w_conv, w_conv)


def _reorder_in_proj(w_in, dn_w, heads2, rw_c, small_w):
    a_end = dn_w + 2 * heads2
    small = jnp.concatenate([w_in[:, a_end + 3 * rw_c:], w_in[:, dn_w:a_end]], axis=1)
    small = jnp.pad(small, ((0, 0), (0, small_w - small.shape[1])))
    return jnp.concatenate([w_in[:, :dn_w], w_in[:, a_end:a_end + 3 * rw_c], small], axis=1)


def _even_mixer(h, w_in, dn_conv, dn_a_log, dn_dt_bias, dn_norm_g,
                rw_mu, rw_w0, rw_w2, rw_a0, rw_a2, rw_g2, rw_k_k, rw_k_a, rw_r_k, rw_ln_g, rw_ln_b,
                s_dn, s_rw):
    bsz, length, dm = h.shape
    dn_heads, dn_dk, dn_dv = s_dn.shape[2:]
    rw_heads, rw_n = s_rw.shape[2:4]
    dn_qk, rw_c = dn_heads * dn_dk, rw_heads * rw_n
    dn_w = 3 * dn_qk + dn_heads * dn_dv
    n_small = w_in.shape[1] - dn_w - 4 * dn_heads - 3 * rw_c
    small_w = 512
    small0 = dn_w + 3 * rw_c
    w_r = _reorder_in_proj(w_in, dn_w, 2 * dn_heads, rw_c, small_w)
    p = matmul(h.reshape(bsz * length, dm), w_r).reshape(bsz, length, -1)
    b_raw = p[..., small0 + n_small:small0 + n_small + 2 * dn_heads]
    a_raw = p[..., small0 + n_small + 2 * dn_heads:small0 + n_small + 4 * dn_heads]
    beta = jax.nn.sigmoid(b_raw).reshape(bsz, length, 2, dn_heads)
    g = -jnp.exp(dn_a_log) * jax.nn.softplus(a_raw.reshape(bsz, length, 2, dn_heads) + dn_dt_bias)
    qkv = delta_prep(p, dn_conv, dn_heads, dn_dk)
    o_f, o_b, s_dn_fin = delta_scan(qkv, jnp.swapaxes(beta, 1, 2), jnp.swapaxes(g, 1, 2), s_dn)
    mu_small = jnp.pad(rw_mu[:, 3 * rw_c:], ((0, 0), (0, small_w - n_small)))
    r, vr, kk, logw, kdir, a_g, gate = rwkv_prep(
        p, dn_w, small0, small_w, rw_mu[:, :3 * rw_c], mu_small, rw_w0, rw_w2, rw_a0, rw_a2, rw_g2,
        rw_k_k, rw_k_a, rw_n)
    y_f, y_b, s_rw_fin = rwkv_scan(r, vr, kk, logw, kdir, a_g, s_rw)
    cat = even_post(o_f, o_b, p, 3 * dn_qk, dn_norm_g, y_f, y_b, r, vr, kdir, gate, rw_r_k.reshape(-1), rw_ln_g, rw_ln_b,
                    rw_n)
    return cat.reshape(bsz * length, -1), s_dn_fin, s_rw_fin


def _odd_mixer(h, w_in, conv_w, conv_b, w_gate, b_gate, lam, s_lru):
    bsz, length, dm = h.shape
    p = matmul(h.reshape(bsz * length, dm), w_in).reshape(bsz, length, -1)
    a, s_fin = lru_mix(p, conv_w, conv_b, w_gate, b_gate, lam, s_lru)
    return a.reshape(bsz * length, -1), s_fin


def _trunk(x, mod, rows, s_dn, s_rw, s_lru, prm):
    depth = mod.shape[0]
    bsz, length, dm = x.shape
    fin_dn, fin_rw, fin_lru = [], [], []
    x = x.reshape(bsz * length, dm)
    h = norm_mod(x, prm["norm_g"][0, 0], mod[0], 0)
    for l in range(depth):
        g = prm["norm_g"][l]
        h = h.reshape(bsz, length, dm)
        i = l // 2
        if l % 2 == 0:
            mix, sd, sr = _even_mixer(
                h, prm["ev_w_in"][i], prm["dn_conv"][i], prm["dn_a_log"][i],
                prm["dn_dt_bias"][i], prm["dn_norm_g"][i], prm["rw_mu"][i], prm["rw_w0"][i],
                prm["rw_w2"][i], prm["rw_a0"][i], prm["rw_a2"][i], prm["rw_g2"][i], prm["rw_k_k"][i],
                prm["rw_k_a"][i], prm["rw_r_k"][i], prm["rw_ln_g"][i], prm["rw_ln_b"][i],
                s_dn[:, i], s_rw[:, i])
            fin_dn.append(sd)
            fin_rw.append(sr)
            w_out = prm["ev_w_out"][i]
        else:
            mix, sl = _odd_mixer(h, prm["od_w_in"][i], prm["lru_conv"][i],
                                 prm["lru_conv_b"][i], prm["lru_w_gate"][i], prm["lru_b_gate"][i],
                                 prm["lru_lambda"][i], s_lru[:, i])
            fin_lru.append(sl)
            w_out = prm["od_w_out"][i]
        x, h = matmul_residual(mix, w_out.astype(BF16), x, g[1], mod[l], 2, next_norm=(g[2], mod[l], 3))
        mid = ffn_up_conv(h, prm["ffn_w_up"][l].astype(BF16), prm["ffn_conv"][l].reshape(9, -1),
                          length // rows, rows)
        nxt = (prm["norm_g"][l + 1, 0], mod[l + 1], 0) if l + 1 < depth else None
        x, h = matmul_residual(mid, prm["ffn_w_down"][l].astype(BF16), x, g[3], mod[l], 5, next_norm=nxt)
    return (x.reshape(bsz, length, dm), jnp.stack(fin_dn, axis=1), jnp.stack(fin_rw, axis=1),
            jnp.stack(fin_lru, axis=1))


def kernel(x_prompt, x_sample, state_dn, state_rwkv, state_lru, c, c_ctx, w_mod, b_mod, norm_g, ffn_w_up, ffn_conv, ffn_w_down, ev_w_in, ev_w_out, dn_conv, dn_a_log, dn_dt_bias, dn_norm_g, rw_mu, rw_w0, rw_w2, rw_a0, rw_a2, rw_g2, rw_k_k, rw_k_a, rw_r_k, rw_ln_g, rw_ln_b, od_w_in, od_w_out, lru_conv, lru_conv_b, lru_w_gate, lru_b_gate, lru_lambda):
    prm = dict(w_mod=w_mod, b_mod=b_mod, norm_g=norm_g, ffn_w_up=ffn_w_up, ffn_conv=ffn_conv,
               ffn_w_down=ffn_w_down, ev_w_in=ev_w_in, ev_w_out=ev_w_out, dn_conv=dn_conv,
               dn_a_log=dn_a_log, dn_dt_bias=dn_dt_bias, dn_norm_g=dn_norm_g, rw_mu=rw_mu, rw_w0=rw_w0,
               rw_w2=rw_w2, rw_a0=rw_a0, rw_a2=rw_a2, rw_g2=rw_g2, rw_k_k=rw_k_k, rw_k_a=rw_k_a,
               rw_r_k=rw_r_k, rw_ln_g=rw_ln_g, rw_ln_b=rw_ln_b, od_w_in=od_w_in, od_w_out=od_w_out,
               lru_conv=lru_conv, lru_conv_b=lru_conv_b, lru_w_gate=lru_w_gate, lru_b_gate=lru_b_gate,
               lru_lambda=lru_lambda)
    bp = x_prompt.shape[0]
    grid_w = 64
    depth, dm = w_mod.shape[0], w_mod.shape[1]
    cond = jnp.concatenate([c_ctx[None, :], c], axis=0)
    mod = modulation_all(jnp.pad(cond, ((0, (-cond.shape[0]) % 16), (0, 0))), w_mod, b_mod)
    mod = mod.reshape(depth, -1, 6, 1, dm)
    y_prompt, new_dn, new_rw, new_lru = _trunk(
        x_prompt, mod[:, :1], 1,
        jnp.zeros((bp,) + state_dn.shape[1:], F32),
        jnp.zeros((bp,) + state_rwkv.shape[1:], F32),
        jnp.zeros((bp,) + state_lru.shape[1:], F32), prm)
    rows = x_sample.shape[1] // grid_w
    y_sample, _, _, _ = _trunk(x_sample, mod[:, 1:1 + c.shape[0]], rows, state_dn, state_rwkv, state_lru,
                               prm)
    return (y_prompt, y_sample, new_dn, new_rw, new_lru)
```

```python
import functools
import math

import jax
import jax.numpy as jnp
from jax import lax
from jax.experimental import pallas as pl
from jax.experimental.pallas import tpu as pltpu

F32 = jnp.float32
BF16 = jnp.bfloat16
HIGHEST = lax.Precision.HIGHEST

NORM_EPS = 1e-6
CHUNK = 64
VMEM_LIMIT_BYTES = 48 * 1024 * 1024
VMEM_LIMIT_LARGE_BYTES = 56 * 1024 * 1024


def _cparams(*sem, vmem=None):
    return pltpu.CompilerParams(dimension_semantics=sem, vmem_limit_bytes=vmem or VMEM_LIMIT_BYTES)


def _dotb(a, b):
    return jnp.dot(a.astype(BF16), b.astype(BF16), preferred_element_type=F32)


def _dotb_nt(a, b):
    return lax.dot_general(a.astype(BF16), b.astype(BF16), (((1,), (1,)), ((), ())),
                           preferred_element_type=F32)


def _dotb_tn(a, b):
    return lax.dot_general(a.astype(BF16), b.astype(BF16), (((0,), (0,)), ((), ())),
                           preferred_element_type=F32)


def _dot_hi(a, b):
    return jnp.dot(a, b, preferred_element_type=F32, precision=HIGHEST)


def _mm_body(a_ref, b_ref, o_ref):
    o_ref[...] = jnp.dot(a_ref[...], b_ref[...], preferred_element_type=F32).astype(o_ref.dtype)


LANES = 128


def _tile(n, target):
    if n <= target:
        return n
    best = None
    for t in range(LANES, target + 1, LANES):
        if n % t == 0:
            best = t
    assert best is not None, n
    return best


def matmul(a, b, out_dtype=F32, tm=1024, tn=1536):
    m, k = a.shape
    n = b.shape[1]
    tm, tn = _tile(m, tm), _tile(n, tn)
    return pl.pallas_call(
        _mm_body,
        grid=(m // tm, n // tn),
        in_specs=[pl.BlockSpec((tm, k), lambda i, j: (i, 0)),
                  pl.BlockSpec((k, tn), lambda i, j: (0, j))],
        out_specs=pl.BlockSpec((tm, tn), lambda i, j: (i, j)),
        out_shape=jax.ShapeDtypeStruct((m, n), out_dtype),
        compiler_params=_cparams("parallel", "arbitrary"),
    )(a.astype(BF16), b.astype(BF16))


def _mod_body(c_ref, w_ref, b_ref, o_ref):
    o_ref[0] = jnp.dot(c_ref[...], w_ref[0].astype(BF16), preferred_element_type=F32) + b_ref[0]


def modulation_all(cvec, w_mod, b_mod, tn=1024):
    nc, dm = cvec.shape
    depth, _, n = w_mod.shape
    return pl.pallas_call(
        _mod_body,
        grid=(depth, n // tn),
        in_specs=[pl.BlockSpec((nc, dm), lambda l, j: (0, 0)),
                  pl.BlockSpec((1, dm, tn), lambda l, j: (l, 0, j)),
                  pl.BlockSpec((1, 1, tn), lambda l, j: (l, 0, j))],
        out_specs=pl.BlockSpec((1, nc, tn), lambda l, j: (l, 0, j)),
        out_shape=jax.ShapeDtypeStruct((depth, nc, n), F32),
        compiler_params=_cparams("parallel", "parallel"),
    )(jax.nn.silu(cvec).astype(BF16), w_mod, b_mod.reshape(depth, 1, n))


def _norm_mod(x, g, scale, shift):
    return x * lax.rsqrt(jnp.mean(x * x, axis=-1, keepdims=True) + NORM_EPS) * g * (1.0 + scale) + shift


def _mod_spec(which, blocks_per_cond, dm):
    return pl.BlockSpec((1, 1, 1, dm), lambda i, *_: (i // blocks_per_cond, which, 0, 0))


def _norm_mod_body(x_ref, g_ref, sc_ref, sh_ref, h_ref):
    h_ref[...] = _norm_mod(x_ref[...], g_ref[...], sc_ref[0, 0], sh_ref[0, 0]).astype(h_ref.dtype)


def norm_mod(x, g, mod, which_shift, tm=512):
    t, dm = x.shape
    bpc = t // mod.shape[0] // tm
    row = pl.BlockSpec((tm, dm), lambda i: (i, 0))
    return pl.pallas_call(
        _norm_mod_body,
        grid=(t // tm,),
        in_specs=[row, pl.BlockSpec((1, dm), lambda i: (0, 0)),
                  _mod_spec(which_shift + 1, bpc, dm), _mod_spec(which_shift, bpc, dm)],
        out_specs=row,
        out_shape=jax.ShapeDtypeStruct((t, dm), BF16),
        compiler_params=_cparams("parallel"),
    )(x, g.reshape(1, dm), mod, mod)


def _mm_res_body(a_ref, w_ref, x_ref, g_ref, gate_ref, *rest, nk, emit_h):
    if emit_h:
        gn_ref, sc_ref, sh_ref, xo_ref, h_ref = rest
    else:
        (xo_ref,) = rest
    k = pl.program_id(1)

    if nk > 1:
        @pl.when(k == 0)
        def _():
            xo_ref[...] = jnp.zeros_like(xo_ref)

        xo_ref[...] += jnp.dot(a_ref[...], w_ref[...], preferred_element_type=F32)

    @pl.when(k == nk - 1)
    def _():
        y = xo_ref[...] if nk > 1 else jnp.dot(a_ref[...], w_ref[...], preferred_element_type=F32)
        y = y * lax.rsqrt(jnp.mean(y * y, axis=-1, keepdims=True) + NORM_EPS) * g_ref[...]
        xn = x_ref[...] + gate_ref[0, 0] * y
        xo_ref[...] = xn
        if emit_h:
            h_ref[...] = _norm_mod(xn, gn_ref[...], sc_ref[0, 0], sh_ref[0, 0]).astype(h_ref.dtype)


def matmul_residual(a, w, x, g_out, mod, which_gate, next_norm=None, tm=512, tk=2048, vmem=None):
    t, kdim = a.shape
    dm = w.shape[1]
    tk = _tile(kdim, tk)
    nk = kdim // tk
    bpc = t // mod.shape[0] // tm
    row = pl.BlockSpec((tm, dm), lambda i, k: (i, 0))
    vec = pl.BlockSpec((1, dm), lambda i, k: (0, 0))
    in_specs = [pl.BlockSpec((tm, tk), lambda i, k: (i, k)),
                pl.BlockSpec((tk, dm), lambda i, k: (k, 0)),
                row, vec, _mod_spec(which_gate, bpc, dm)]
    args = [a, w, x, g_out.reshape(1, dm), mod]
    out_specs = [row]
    out_shape = [jax.ShapeDtypeStruct((t, dm), F32)]
    if next_norm is not None:
        g_next, mod_next, which_shift = next_norm
        in_specs += [vec, _mod_spec(which_shift + 1, bpc, dm), _mod_spec(which_shift, bpc, dm)]
        args += [g_next.reshape(1, dm), mod_next, mod_next]
        out_specs.append(row)
        out_shape.append(jax.ShapeDtypeStruct((t, dm), BF16))
    out = pl.pallas_call(
        functools.partial(_mm_res_body, nk=nk, emit_h=next_norm is not None),
        grid=(t // tm, nk),
        in_specs=in_specs,
        out_specs=out_specs,
        out_shape=out_shape,
        compiler_params=_cparams("parallel", "arbitrary", vmem=vmem),
    )(*args)
    return out if next_norm is not None else (out[0], None)


def _order_masks(c, rev):
    ii = lax.broadcasted_iota(jnp.int32, (c, c), 0)
    jj = lax.broadcasted_iota(jnp.int32, (c, c), 1)
    lo = jnp.where(rev, jj, ii)
    hi = jnp.where(rev, ii, jj)
    return ii, jj, lo, hi


def _unit_tri_inverses(ms, ii, jj, lo, hi, which=None):
    c = ms[0].shape[0]
    eye = (ii == jj).astype(F32)
    if which is None:
        lo, hi, which = [lo], [hi], [0] * len(ms)

    def pair_masks(shift):
        out = []
        for l, h in zip(lo, hi):
            same = (l >> (shift + 1)) == (h >> (shift + 1))
            out.append(same & (((l >> shift) & 1) == 1) & (((h >> shift) & 1) == 0))
        return out

    masks = pair_masks(0)
    ts = [eye - jnp.where(masks[w], m, 0.0) for m, w in zip(ms, which)]
    shift = 1
    while (1 << shift) < c:
        masks = pair_masks(shift)
        tb = [_dotb(t, jnp.where(masks[w], m, 0.0)) for t, m, w in zip(ts, ms, which)]
        ts = [t - _dotb(x, t) for x, t in zip(tb, ts)]
        shift += 1
    return ts


def _delta_body(qf_ref, kf_ref, vf_ref, qb_ref, kb_ref, vb_ref, betaf_ref, betab_ref, gf_ref, gb_ref,
                gtf_ref, gtb_ref, s0_ref, of_ref, ob_ref, s_ref, *, heads, dk, dv):
    c = qf_ref.shape[1]

    @pl.when(pl.program_id(1) == 0)
    def _():
        s_ref[...] = s0_ref[...]

    ii, jj = _order_masks(c, False)[:2]
    los, his = [ii, jj], [jj, ii]
    units = [(d, h) for d in range(2) for h in range(heads)]
    q_refs, k_refs, v_refs = (qf_ref, qb_ref), (kf_ref, kb_ref), (vf_ref, vb_ref)
    gc_col, gc_row, g_tot, beta, incl, strict = [], [], [], [], [], []
    for d, (g_ref, gt_ref, b_ref) in enumerate(((gf_ref, gtf_ref, betaf_ref), (gb_ref, gtb_ref, betab_ref))):
        incl.append(los[d] >= his[d])
        strict.append(los[d] > his[d])
        m_incl = incl[d].astype(F32)
        g = g_ref[0, 0]
        gc_col.append(_dot_hi(m_incl, g))
        gc_row.append(lax.dot_general(gt_ref[0, 0, 0], m_incl, (((1,), (1,)), ((), ())),
                                      preferred_element_type=F32, precision=HIGHEST))
        g_tot.append(jnp.sum(g, axis=0, keepdims=True))
        beta.append(b_ref[0, 0])

    q = [q_refs[d][0, :, h * dk:(h + 1) * dk] for d, h in units]
    k = [k_refs[d][0, :, h * dk:(h + 1) * dk] for d, h in units]
    v = [v_refs[d][0, :, h * dv:(h + 1) * dv] for d, h in units]
    s = [s_ref[0, d, h] for d, h in units]
    gcc = [gc_col[d][:, h:h + 1] for d, h in units]
    bc = [beta[d][:, h:h + 1] for d, h in units]
    gt = [g_tot[d][:, h:h + 1] for d, h in units]
    us = range(len(units))
    dec_incl = []
    for i, (d, h) in enumerate(units):
        diff = gcc[i] - gc_row[d][h:h + 1, :]
        dec_incl.append(jnp.where(incl[d], jnp.exp(jnp.where(incl[d], diff, 0.0)), 0.0))
    kk = [_dotb_nt(k[i], k[i]) for i in us]
    qk = [_dotb_nt(q[i], k[i]) * dec_incl[i] for i in us]
    m = [bc[i] * kk[i] * jnp.where(strict[units[i][0]], dec_incl[i], 0.0) for i in us]
    t = _unit_tri_inverses(m, ii, jj, los, his, which=[d for d, _ in units])
    egc = [jnp.exp(gcc[i]) for i in us]
    sol = [_dotb(t[i], jnp.concatenate([bc[i] * v[i], (bc[i] * egc[i]) * k[i]], axis=1)) for i in us]
    qs = [_dotb(q[i] * egc[i], s[i]) for i in us]
    u = [sol[i][:, :dv] - _dotb(sol[i][:, dv:], s[i]) for i in us]
    o = [qs[i] + _dotb(qk[i], u[i]) for i in us]
    s_new = [jnp.exp(gt[i]) * s[i] + _dotb_tn(k[i] * jnp.exp(gt[i] - gcc[i]), u[i]) for i in us]
    o_refs = (of_ref, ob_ref)
    for i, (d, h) in enumerate(units):
        s_ref[0, d, h] = s_new[i]
        o_refs[d][0, 0, :, h * dv:(h + 1) * dv] = o[i]


def delta_scan(qkv, beta, g, s0):
    bsz, length, _ = qkv.shape
    _, _, heads, dk, dv = s0.shape
    assert dk == dv
    c = CHUNK
    n = length // c
    gt = jnp.swapaxes(g.reshape(bsz, 2, n, c, heads), 3, 4)

    def chunk_of(d, i):
        return i + d * (n - 1 - 2 * i)

    tok = lambda part, d: pl.BlockSpec((1, c, heads * dk), lambda b, i: (b, chunk_of(d, i), part))
    par = lambda d: pl.BlockSpec((1, 1, c, heads), lambda b, i: (b, d, chunk_of(d, i), 0))
    row = lambda d: pl.BlockSpec((1, 1, 1, heads, c), lambda b, i: (b, d, chunk_of(d, i), 0, 0))
    st = pl.BlockSpec((1, 2, heads, dk, dv), lambda b, i: (b, 0, 0, 0, 0))
    out = lambda d: pl.BlockSpec((1, 1, c, heads * dv), lambda b, i: (0, b, chunk_of(d, i), 0))
    return pl.pallas_call(
        functools.partial(_delta_body, heads=heads, dk=dk, dv=dv),
        grid=(bsz, n),
        in_specs=[tok(0, 0), tok(1, 0), tok(2, 0), tok(0, 1), tok(1, 1), tok(2, 1),
                  par(0), par(1), par(0), par(1), row(0), row(1), st],
        out_specs=[out(0), out(1), st],
        out_shape=[jax.ShapeDtypeStruct((1, bsz, length, heads * dv), F32),
                   jax.ShapeDtypeStruct((1, bsz, length, heads * dv), F32),
                   jax.ShapeDtypeStruct(s0.shape, F32)],
        compiler_params=_cparams("parallel", "arbitrary"),
    )(qkv, qkv, qkv, qkv, qkv, qkv, beta, beta, g, g, gt, gt, s0)


def _shift_rows(x, s, rows):
    length = x.shape[0]
    y = pltpu.roll(x, s % length, 0)
    return jnp.where(rows >= s if s > 0 else rows < length + s, y, 0.0)


def _short_conv(x, w_ref, rows):
    return (w_ref[0:1, :] * _shift_rows(x, 2, rows) + w_ref[1:2, :] * _shift_rows(x, 1, rows)
            + w_ref[2:3, :] * x + w_ref[3:4, :] * _shift_rows(x, -1, rows))


def _token_shift(x, mu_ref, rows):
    return (x + mu_ref[0:1, :] * (_shift_rows(x, 1, rows) - x)
            + mu_ref[1:2, :] * (_shift_rows(x, -1, rows) - x))


def _group_ones(width, group):
    i = lax.broadcasted_iota(jnp.int32, (width, width), 0) // group
    j = lax.broadcasted_iota(jnp.int32, (width, width), 1) // group
    return (i == j).astype(BF16)


def _group_sum(x, ones):
    hi = x.astype(BF16)
    lo = (x - hi.astype(F32)).astype(BF16)
    return (jnp.dot(hi, ones, preferred_element_type=F32) + jnp.dot(lo, ones, preferred_element_type=F32))


def _delta_prep_body(p_ref, w_ref, o_ref, *, dk, q_blocks):
    j = pl.program_id(1)
    rows = lax.broadcasted_iota(jnp.int32, p_ref.shape[1:], 0)
    y = jax.nn.silu(_short_conv(p_ref[0], w_ref, rows))
    norm_w = jnp.where(j < q_blocks, dk ** -0.5, jnp.where(j < 2 * q_blocks, 1.0, 0.0))
    plain_w = jnp.where(j < 2 * q_blocks, 0.0, 1.0)
    for h in range(y.shape[1] // dk):
        yh = y[:, h * dk:(h + 1) * dk]
        inv = lax.rsqrt(jnp.sum(yh * yh, axis=-1, keepdims=True) + 1e-6)
        o_ref[0, :, h * dk:(h + 1) * dk] = yh * (inv * norm_w + plain_w)


def delta_prep(p, conv_w, heads, dk, tc=512):
    bsz, length, _ = p.shape
    width = heads * dk
    spec = pl.BlockSpec((1, length, tc), lambda b, j: (b, 0, j))
    return pl.pallas_call(
        functools.partial(_delta_prep_body, dk=dk, q_blocks=width // tc),
        grid=(bsz, 3 * width // tc),
        in_specs=[spec, pl.BlockSpec((4, tc), lambda b, j: (0, j))],
        out_specs=spec,
        out_shape=jax.ShapeDtypeStruct((bsz, length, 3 * width), F32),
        compiler_params=_cparams("parallel", "parallel"),
    )(p, conv_w)


def _rwkv_prep_body(r_ref, k_ref, v_ref, sm_ref, mur_ref, muk_ref, muv_ref, mus_ref,
                    w0_ref, w2_ref, a0_ref, a2_ref, g2_ref, kkw_ref, kaw_ref,
                    ro_ref, vo_ref, kk_ref, lw_ref, kd_ref, ag_ref, gate_ref, *, hn, r_decay, r_a, r_gate):
    rows = lax.broadcasted_iota(jnp.int32, r_ref.shape[1:], 0)
    rows_s = lax.broadcasted_iota(jnp.int32, sm_ref.shape[1:], 0)
    ro_ref[0] = _token_shift(r_ref[0], mur_ref, rows)
    vo_ref[0] = _token_shift(v_ref[0], muv_ref, rows)
    kr = _token_shift(k_ref[0], muk_ref, rows)
    sm = _token_shift(sm_ref[0], mus_ref, rows_s)
    wd = jnp.tanh(sm[:, :2 * r_decay])
    ad = sm[:, 2 * r_decay:2 * r_decay + 2 * r_a]
    gd = jax.nn.sigmoid(sm[:, 2 * r_decay + 2 * r_a:2 * r_decay + 2 * r_a + r_gate])
    gate_ref[0] = _dotb(gd, g2_ref[...])
    kx = kr * kkw_ref[...]
    ones = _group_ones(kx.shape[1], hn)
    kk_ref[0] = kx * lax.rsqrt(_group_sum(kx * kx, ones) + 1e-6)
    for d in range(2):
        w_pre = w0_ref[d:d + 1, :] + _dotb(wd[:, d * r_decay:(d + 1) * r_decay], w2_ref[d])
        lw_ref[0, d] = -jnp.exp(-jax.nn.softplus(-w_pre) - 0.5)
        a_g = jax.nn.sigmoid(a0_ref[d:d + 1, :] + _dotb(ad[:, d * r_a:(d + 1) * r_a], a2_ref[d]))
        ag_ref[0, d] = a_g
        kd_ref[0, d] = kr * (1.0 + (a_g - 1.0) * kaw_ref[...])


def rwkv_prep(p, col0, small_col0, small_w, mu_rkv, mu_small, w0, w2, a0, a2, g2, k_k, k_a, hn, tc=256):
    bsz, length, _ = p.shape
    width = w0.shape[1]
    r_decay, r_a, r_gate = w2.shape[1], a2.shape[1], g2.shape[0]
    nt = width // tc
    assert col0 % tc == 0 and small_col0 % small_w == 0 and width % tc == 0

    def part(i):
        return pl.BlockSpec((1, length, tc), lambda b, j: (b, 0, col0 // tc + i * nt + j))

    def mu_part(i):
        return pl.BlockSpec((2, tc), lambda b, j: (0, i * nt + j))

    vec = pl.BlockSpec((1, tc), lambda b, j: (0, j))
    dvec = pl.BlockSpec((2, tc), lambda b, j: (0, j))
    tok = pl.BlockSpec((1, length, tc), lambda b, j: (b, 0, j))
    dtok = pl.BlockSpec((1, 2, length, tc), lambda b, j: (b, 0, 0, j))
    one = jax.ShapeDtypeStruct((bsz, length, width), F32)
    two = jax.ShapeDtypeStruct((bsz, 2, length, width), F32)
    return pl.pallas_call(
        functools.partial(_rwkv_prep_body, hn=hn, r_decay=r_decay, r_a=r_a, r_gate=r_gate),
        grid=(bsz, nt),
        in_specs=[part(0), part(1), part(2),
                  pl.BlockSpec((1, length, small_w), lambda b, j: (b, 0, small_col0 // small_w)),
                  mu_part(0), mu_part(1), mu_part(2),
                  pl.BlockSpec((2, small_w), lambda b, j: (0, 0)),
                  dvec, pl.BlockSpec((2, r_decay, tc), lambda b, j: (0, 0, j)),
                  dvec, pl.BlockSpec((2, r_a, tc), lambda b, j: (0, 0, j)),
                  pl.BlockSpec((r_gate, tc), lambda b, j: (0, j)), vec, vec],
        out_specs=[tok, tok, tok, dtok, dtok, dtok, tok],
        out_shape=[one, one, one, two, two, two, one],
        compiler_params=_cparams("parallel", "parallel"),
    )(p, p, p, p, mu_rkv, mu_rkv, mu_rkv, mu_small, w0, w2, a0, a2, g2,
      k_k.reshape(1, width), k_a.reshape(1, width))


def _even_post_body(of_ref, ob_ref, z_ref, dg_ref, yf_ref, yb_ref, r_ref, v_ref, kf_ref, kb_ref,
                    gate_ref, rk_ref, lg_ref, lb_ref, out_ref, *, dv, hn, gn_eps):
    o = of_ref[0, 0] + ob_ref[0, 0]
    wa = o.shape[1]
    z = z_ref[0]
    for h in range(wa // dv):
        sl = slice(h * dv, (h + 1) * dv)
        oh = o[:, sl]
        oh = oh * lax.rsqrt(jnp.mean(oh * oh, axis=-1, keepdims=True) + NORM_EPS) * dg_ref[...]
        out_ref[0, :, sl] = (oh * jax.nn.silu(z[:, sl])).astype(out_ref.dtype)
    y = yf_ref[0, 0] + yb_ref[0, 0]
    lane = 128
    ones = _group_ones(lane, hn)
    for t in range(y.shape[1] // lane):
        sl = slice(t * lane, (t + 1) * lane)
        yt = y[:, sl]
        cen = yt - _group_sum(yt, ones) * (1.0 / hn)
        var = _group_sum(cen * cen, ones) * (1.0 / hn)
        yn = cen * lax.rsqrt(var + gn_eps) * lg_ref[:, sl] + lb_ref[:, sl]
        rk = r_ref[0, :, sl] * (kf_ref[0, 0, :, sl] + kb_ref[0, 0, :, sl]) * rk_ref[:, sl]
        bonus = _group_sum(rk, ones) * v_ref[0, :, sl]
        out_ref[0, :, wa + t * lane:wa + (t + 1) * lane] = ((yn + bonus) * gate_ref[0, :, sl]).astype(
            out_ref.dtype)


def even_post(o_f, o_b, p, z_col, dn_norm_g, y_f, y_b, r, v, kdir, gate, r_k, ln_g, ln_b, hn, tl=256):
    _, bsz, length, wa = o_f.shape
    lead = lambda w: pl.BlockSpec((1, 1, tl, w), lambda b, i: (0, b, i, 0))
    wb = y_f.shape[3]
    dv = dn_norm_g.shape[0]
    assert z_col % wa == 0
    dirs = lambda d, w: pl.BlockSpec((1, 1, tl, w), lambda b, i: (b, d, i, 0))
    tok = lambda w: pl.BlockSpec((1, tl, w), lambda b, i: (b, i, 0))
    vec = lambda w: pl.BlockSpec((1, w), lambda b, i: (0, 0))
    return pl.pallas_call(
        functools.partial(_even_post_body, dv=dv, hn=hn, gn_eps=hn * 1e-5),
        grid=(bsz, length // tl),
        in_specs=[lead(wa), lead(wa),
                  pl.BlockSpec((1, tl, wa), lambda b, i: (b, i, z_col // wa)), vec(dv),
                  lead(wb), lead(wb), tok(wb), tok(wb), dirs(0, wb), dirs(1, wb), tok(wb),
                  vec(wb), vec(wb), vec(wb)],
        out_specs=tok(wa + wb),
        out_shape=jax.ShapeDtypeStruct((bsz, length, wa + wb), BF16),
        compiler_params=_cparams("parallel", "parallel"),
    )(o_f, o_b, p, dn_norm_g.reshape(1, dv), y_f, y_b, r, v, kdir, kdir, gate,
      r_k.reshape(1, wb), ln_g.reshape(1, wb), ln_b.reshape(1, wb))


def _rwkv_body(rf_ref, vf_ref, kkf_ref, rb_ref, vb_ref, kkb_ref, lwf_ref, lwb_ref, kdf_ref, kdb_ref,
               agf_ref, agb_ref, s0_ref, yf_ref, yb_ref, s_ref, *, heads, hn):
    c = rf_ref.shape[1]

    @pl.when(pl.program_id(1) == 0)
    def _():
        s_ref[...] = s0_ref[...]

    ii, jj = _order_masks(c, False)[:2]
    los, his = [ii, jj], [jj, ii]
    i2 = lax.broadcasted_iota(jnp.int32, (c, 2 * c), 0)
    j2 = lax.broadcasted_iota(jnp.int32, (c, 2 * c), 1) & (c - 1)
    incl2 = [i2 >= j2, j2 >= i2]
    strict = [ii > jj, jj > ii]
    a_t, r_t, b_t, k_t, b_p, k_p, p_c, v = [], [], [], [], [], [], [], []
    dir_refs = ((rf_ref, vf_ref, kkf_ref, lwf_ref, kdf_ref, agf_ref),
                (rb_ref, vb_ref, kkb_ref, lwb_ref, kdb_ref, agb_ref))
    for d, (r_ref, v_ref, kk_ref, lw_ref, kd_ref, ag_ref) in enumerate(dir_refs):
        logw = lw_ref[0, 0]
        lw = _dot_hi((los[d] >= his[d]).astype(F32), logw)
        lw_tot = jnp.sum(logw, axis=0, keepdims=True)
        kk = kk_ref[0]
        kd = kd_ref[0, 0]
        bb = kk * ag_ref[0, 0]
        e_out = jnp.exp(-lw)
        e_rem = jnp.exp(lw_tot - lw)
        a_t.append(-kk * jnp.exp(lw - logw))
        r_t.append(r_ref[0] * jnp.exp(lw))
        b_t.append(bb * e_out)
        k_t.append(kd * e_out)
        b_p.append(bb * e_rem)
        k_p.append(kd * e_rem)
        p_c.append(jnp.exp(lw_tot))
        v.append(v_ref[0])

    units = [(d, h) for d in range(2) for h in range(heads)]
    us = range(len(units))
    sl = [slice(h * hn, (h + 1) * hn) for _, h in units]
    dr = [d for d, _ in units]
    s = [s_ref[0, d, h] for d, h in units]
    lhs = [jnp.concatenate([a_t[dr[i]][:, sl[i]], r_t[dr[i]][:, sl[i]]], axis=0) for i in us]
    rhs = [jnp.concatenate([b_t[dr[i]][:, sl[i]], k_t[dr[i]][:, sl[i]]], axis=0) for i in us]
    gram = [_dotb_nt(lhs[i], rhs[i]) for i in us]
    ls = [_dotb_nt(lhs[i], s[i]) for i in us]
    t = _unit_tri_inverses([jnp.where(strict[dr[i]], -gram[i][:c, :c], 0.0) for i in us],
                           ii, jj, los, his, which=dr)
    vh = [v[dr[i]][:, sl[i]] for i in us]
    pre = [ls[i][:c] + _dotb(jnp.where(strict[dr[i]], gram[i][:c, c:], 0.0), vh[i]) for i in us]
    uv = [jnp.concatenate([_dotb(t[i], pre[i]), vh[i]], axis=0) for i in us]
    y = [ls[i][c:] + _dotb(jnp.where(incl2[dr[i]], gram[i][c:], 0.0), uv[i]) for i in us]
    s_new = [s[i] * p_c[dr[i]][:, sl[i]]
             + _dotb_tn(uv[i], jnp.concatenate([b_p[dr[i]][:, sl[i]], k_p[dr[i]][:, sl[i]]], axis=0))
             for i in us]
    y_refs = (yf_ref, yb_ref)
    for i, (d, h) in enumerate(units):
        s_ref[0, d, h] = s_new[i]
        y_refs[d][0, 0, :, sl[i]] = y[i]


def rwkv_scan(r, v, kk, logw, kdir, ag, s0):
    bsz, length, width = r.shape
    _, _, heads, hn, _ = s0.shape
    c = CHUNK
    n = length // c

    def chunk_of(d, i):
        return i + d * (n - 1 - 2 * i)

    tok = lambda d: pl.BlockSpec((1, c, width), lambda b, i: (b, chunk_of(d, i), 0))
    par = lambda d: pl.BlockSpec((1, 1, c, width), lambda b, i: (b, d, chunk_of(d, i), 0))
    out = lambda d: pl.BlockSpec((1, 1, c, width), lambda b, i: (0, b, chunk_of(d, i), 0))
    st = pl.BlockSpec((1, 2, heads, hn, hn), lambda b, i: (b, 0, 0, 0, 0))
    one_dir = jax.ShapeDtypeStruct((1, bsz, length, width), F32)
    return pl.pallas_call(
        functools.partial(_rwkv_body, heads=heads, hn=hn),
        grid=(bsz, n),
        in_specs=[tok(0), tok(0), tok(0), tok(1), tok(1), tok(1),
                  par(0), par(1), par(0), par(1), par(0), par(1), st],
        out_specs=[out(0), out(1), st],
        out_shape=[one_dir, one_dir, jax.ShapeDtypeStruct(s0.shape, F32)],
        compiler_params=_cparams("parallel", "arbitrary"),
    )(r, v, kk, r, v, kk, logw, logw, kdir, kdir, ag, ag, s0)


LRU_C = 8.0


def _scan_pass(a, b, shift, rows, rev):
    length = a.shape[0]
    if rev:
        a_s = pltpu.roll(a, length - shift, 0)
        b_s = pltpu.roll(b, length - shift, 0)
        valid = rows < length - shift
    else:
        a_s = pltpu.roll(a, shift, 0)
        b_s = pltpu.roll(b, shift, 0)
        valid = rows >= shift
    a_s = jnp.where(valid, a_s, 1.0)
    b_s = jnp.where(valid, b_s, 0.0)
    return a * a_s, a * b_s + b


SUBLANES = 8


def _blocked_scan(a, b, rows, rev, a_ref, b_ref):
    length, width = a.shape
    sub = rows & (SUBLANES - 1)
    for shift in (1, 2, 4):
        if rev:
            a_s, b_s = pltpu.roll(a, length - shift, 0), pltpu.roll(b, length - shift, 0)
            valid = sub < SUBLANES - shift
        else:
            a_s, b_s = pltpu.roll(a, shift, 0), pltpu.roll(b, shift, 0)
            valid = sub >= shift
        a, b = a * jnp.where(valid, a_s, 1.0), a * jnp.where(valid, b_s, 0.0) + b
    a_ref[...] = a
    b_ref[...] = b
    nb = length // SUBLANES
    last = 0 if rev else SUBLANES - 1
    a_blk = a_ref[pl.ds(last, nb, stride=SUBLANES), :]
    b_blk = b_ref[pl.ds(last, nb, stride=SUBLANES), :]
    brow = lax.broadcasted_iota(jnp.int32, (nb, width), 0)
    shift = 1
    while shift < nb:
        a_blk, b_blk = _scan_pass(a_blk, b_blk, shift, brow, rev)
        shift *= 2
    carry = _shift_rows(b_blk, -1 if rev else 1, brow)
    carry = jnp.broadcast_to(carry[:, None, :], (nb, SUBLANES, width)).reshape(length, width)
    return a * carry + b


def _lru_body(x_ref, gbr_ref, cw_ref, cb_ref, wg_ref, bg_ref, lam_ref, s0_ref, o_ref, s_ref, a_scr, b_scr):
    rows = lax.broadcasted_iota(jnp.int32, x_ref.shape[1:], 0)
    x = _short_conv(x_ref[0], cw_ref, rows) + cb_ref[...]
    length = x.shape[0]
    xb = x.astype(BF16)
    y = jnp.zeros_like(x)
    for d in range(2):
        rev = d == 1
        r_g = jax.nn.sigmoid(jnp.dot(xb, wg_ref[d, 0, 0].astype(BF16), preferred_element_type=F32)
                             + bg_ref[d, 0:1, :])
        i_g = jax.nn.sigmoid(jnp.dot(xb, wg_ref[d, 1, 0].astype(BF16), preferred_element_type=F32)
                             + bg_ref[d, 1:2, :])
        lam = lam_ref[d:d + 1, :]
        log_a = -LRU_C * r_g * jax.nn.softplus(-lam)
        a = jnp.exp(log_a)
        b = jnp.sqrt(-jnp.tanh(log_a) * (a * a + 1.0)) * (i_g * x)
        first = (rows == (length - 1 if rev else 0))
        b = jnp.where(first, b + a * s0_ref[0, d:d + 1, :], b)
        b = _blocked_scan(a, b, rows, rev, a_scr, b_scr)
        y = y + b
        s_ref[0, d:d + 1, :] = b[0:1, :] if rev else b[length - 1:length, :]
    o_ref[0] = (jax.nn.gelu(gbr_ref[0]) * y).astype(o_ref.dtype)


def lru_mix(p, conv_w, conv_b, w_gate, b_gate, lam, s0):
    bsz, length, _ = p.shape
    nb, bs = w_gate.shape[2], w_gate.shape[3]
    width = nb * bs
    tok = pl.BlockSpec((1, length, bs), lambda b, j: (b, 0, j))
    st = pl.BlockSpec((1, 2, bs), lambda b, j: (b, 0, j))
    return pl.pallas_call(
        _lru_body,
        grid=(bsz, nb),
        in_specs=[pl.BlockSpec((1, length, bs), lambda b, j: (b, 0, nb + j)), tok,
                  pl.BlockSpec((4, bs), lambda b, j: (0, j)),
                  pl.BlockSpec((1, bs), lambda b, j: (0, j)),
                  pl.BlockSpec((2, 2, 1, bs, bs), lambda b, j: (0, 0, j, 0, 0)),
                  pl.BlockSpec((2, 2, bs), lambda b, j: (0, 0, j)),
                  pl.BlockSpec((2, bs), lambda b, j: (0, j)),
                  st],
        out_specs=[tok, st],
        out_shape=[jax.ShapeDtypeStruct((bsz, length, width), BF16),
                   jax.ShapeDtypeStruct((bsz, 2, width), F32)],
        scratch_shapes=[pltpu.VMEM((length, bs), F32)] * 2,
        compiler_params=_cparams("parallel", "parallel"),
    )(p, p, conv_w, conv_b.reshape(1, width), w_gate, b_gate, lam, s0)


def _dwconv(x, w9, cols, grid_w, image_rows):
    tokens = x.shape[0]
    w = lambda i, j: w9[3 * i + j:3 * i + j + 1, :]
    x_l = pltpu.roll(jnp.where(cols != grid_w - 1, x, 0.0), 1, 0)
    x_r = pltpu.roll(jnp.where(cols != 0, x, 0.0), tokens - 1, 0)
    out = w(1, 0) * x_l + w(1, 1) * x + w(1, 2) * x_r
    if image_rows > 1:
        assert tokens == grid_w * image_rows
        up = w(0, 0) * x_l + w(0, 1) * x + w(0, 2) * x_r
        dn = w(2, 0) * x_l + w(2, 1) * x + w(2, 2) * x_r
        zero = jnp.zeros((grid_w, x.shape[1]), x.dtype)
        out = out + jnp.concatenate([zero, up[:tokens - grid_w]], axis=0)
        out = out + jnp.concatenate([dn[grid_w:], zero], axis=0)
    return out


FFN_ROW_CHUNK = 256


def _ffn_up_body(h_ref, wa_ref, wb_ref, ca_ref, cb_ref, o_ref, *, grid_w, image_rows):
    cols = lax.broadcasted_iota(jnp.int32, o_ref.shape, 0) & (grid_w - 1)
    chunks = [h_ref[r:r + FFN_ROW_CHUNK, :] for r in range(0, h_ref.shape[0], FFN_ROW_CHUNK)]
    ua = jnp.concatenate([jnp.dot(hc, wa_ref[...], preferred_element_type=F32) for hc in chunks], axis=0)
    ub = jnp.concatenate([jnp.dot(hc, wb_ref[...], preferred_element_type=F32) for hc in chunks], axis=0)
    ca = _dwconv(ua, ca_ref[...], cols, grid_w, image_rows)
    cb = _dwconv(ub, cb_ref[...], cols, grid_w, image_rows)
    o_ref[...] = (jax.nn.silu(ca) * cb).astype(o_ref.dtype)


def ffn_up_conv(h, w_up, w_conv, grid_w, image_rows, tb=1024, tc=512):
    t, dm = h.shape
    f = w_up.shape[1] // 2
    nf = f // tc
    assert t % tb == 0 and f % tc == 0 and tb % (grid_w * image_rows) == 0
    assert image_rows == 1 or tb == grid_w * image_rows
    return pl.pallas_call(
        functools.partial(_ffn_up_body, grid_w=grid_w, image_rows=image_rows),
        grid=(t // tb, nf),
        in_specs=[pl.BlockSpec((tb, dm), lambda i, j: (i, 0)),
                  pl.BlockSpec((dm, tc), lambda i, j: (0, j)),
                  pl.BlockSpec((dm, tc), lambda i, j: (0, j + nf)),
                  pl.BlockSpec((9, tc), lambda i, j: (0, j)),
                  pl.BlockSpec((9, tc), lambda i, j: (0, j + nf))],
        out_specs=pl.BlockSpec((tb, tc), lambda i, j: (i, j)),
        out_shape=jax.ShapeDtypeStruct((t, f), BF16),
        compiler_params=_cparams("parallel", "arbitrary"),
    )(h, w_up, w_up, w_conv, w_conv)


def _reorder_in_proj(w_in, dn_w, heads2, rw_c, small_w):
    a_end = dn_w + 2 * heads2
    small = jnp.concatenate([w_in[:, a_end + 3 * rw_c:], w_in[:, dn_w:a_end]], axis=1)
    small = jnp.pad(small, ((0, 0), (0, small_w - small.shape[1])))
    return jnp.concatenate([w_in[:, :dn_w], w_in[:, a_end:a_end + 3 * rw_c], small], axis=1)


def _even_mixer(h, w_in, dn_conv, dn_a_log, dn_dt_bias, dn_norm_g,
                rw_mu, rw_w0, rw_w2, rw_a0, rw_a2, rw_g2, rw_k_k, rw_k_a, rw_r_k, rw_ln_g, rw_ln_b,
                s_dn, s_rw):
    bsz, length, dm = h.shape
    dn_heads, dn_dk, dn_dv = s_dn.shape[2:]
    rw_heads, rw_n = s_rw.shape[2:4]
    dn_qk, rw_c = dn_heads * dn_dk, rw_heads * rw_n
    dn_w = 3 * dn_qk + dn_heads * dn_dv
    n_small = w_in.shape[1] - dn_w - 4 * dn_heads - 3 * rw_c
    small_w = 512
    small0 = dn_w + 3 * rw_c
    w_r = _reorder_in_proj(w_in, dn_w, 2 * dn_heads, rw_c, small_w)
    p = matmul(h.reshape(bsz * length, dm), w_r).reshape(bsz, length, -1)
    b_raw = p[..., small0 + n_small:small0 + n_small + 2 * dn_heads]
    a_raw = p[..., small0 + n_small + 2 * dn_heads:small0 + n_small + 4 * dn_heads]
    beta = jax.nn.sigmoid(b_raw).reshape(bsz, length, 2, dn_heads)
    g = -jnp.exp(dn_a_log) * jax.nn.softplus(a_raw.reshape(bsz, length, 2, dn_heads) + dn_dt_bias)
    qkv = delta_prep(p, dn_conv, dn_heads, dn_dk)
    o_f, o_b, s_dn_fin = delta_scan(qkv, jnp.swapaxes(beta, 1, 2), jnp.swapaxes(g, 1, 2), s_dn)
    mu_small = jnp.pad(rw_mu[:, 3 * rw_c:], ((0, 0), (0, small_w - n_small)))
    r, vr, kk, logw, kdir, a_g, gate = rwkv_prep(
        p, dn_w, small0, small_w, rw_mu[:, :3 * rw_c], mu_small, rw_w0, rw_w2, rw_a0, rw_a2, rw_g2,
        rw_k_k, rw_k_a, rw_n)
    y_f, y_b, s_rw_fin = rwkv_scan(r, vr, kk, logw, kdir, a_g, s_rw)
    cat = even_post(o_f, o_b, p, 3 * dn_qk, dn_norm_g, y_f, y_b, r, vr, kdir, gate, rw_r_k.reshape(-1), rw_ln_g, rw_ln_b,
                    rw_n)
    return cat.reshape(bsz * length, -1), s_dn_fin, s_rw_fin


def _odd_mixer(h, w_in, conv_w, conv_b, w_gate, b_gate, lam, s_lru):
    bsz, length, dm = h.shape
    p = matmul(h.reshape(bsz * length, dm), w_in).reshape(bsz, length, -1)
    a, s_fin = lru_mix(p, conv_w, conv_b, w_gate, b_gate, lam, s_lru)
    return a.reshape(bsz * length, -1), s_fin


def _trunk(x, mod, rows, s_dn, s_rw, s_lru, prm):
    depth = mod.shape[0]
    bsz, length, dm = x.shape
    fin_dn, fin_rw, fin_lru = [], [], []
    x = x.reshape(bsz * length, dm)
    h = norm_mod(x, prm["norm_g"][0, 0], mod[0], 0)
    for l in range(depth):
        g = prm["norm_g"][l]
        h = h.reshape(bsz, length, dm)
        i = l // 2
        if l % 2 == 0:
            mix, sd, sr = _even_mixer(
                h, prm["ev_w_in"][i], prm["dn_conv"][i], prm["dn_a_log"][i],
                prm["dn_dt_bias"][i], prm["dn_norm_g"][i], prm["rw_mu"][i], prm["rw_w0"][i],
                prm["rw_w2"][i], prm["rw_a0"][i], prm["rw_a2"][i], prm["rw_g2"][i], prm["rw_k_k"][i],
                prm["rw_k_a"][i], prm["rw_r_k"][i], prm["rw_ln_g"][i], prm["rw_ln_b"][i],
                s_dn[:, i], s_rw[:, i])
            fin_dn.append(sd)
            fin_rw.append(sr)
            w_out = prm["ev_w_out"][i]
        else:
            mix, sl = _odd_mixer(h, prm["od_w_in"][i], prm["lru_conv"][i],
                                 prm["lru_conv_b"][i], prm["lru_w_gate"][i], prm["lru_b_gate"][i],
                                 prm["lru_lambda"][i], s_lru[:, i])
            fin_lru.append(sl)
            w_out = prm["od_w_out"][i]
        x, h = matmul_residual(mix, w_out.astype(BF16), x, g[1], mod[l], 2, next_norm=(g[2], mod[l], 3))
        mid = ffn_up_conv(h, prm["ffn_w_up"][l].astype(BF16), prm["ffn_conv"][l].reshape(9, -1),
                          length // rows, rows)
        nxt = (prm["norm_g"][l + 1, 0], mod[l + 1], 0) if l + 1 < depth else None
        x, h = matmul_residual(mid, prm["ffn_w_down"][l].astype(BF16), x, g[3], mod[l], 5, next_norm=nxt,
                               tm=1024, tk=512, vmem=VMEM_LIMIT_LARGE_BYTES)
    return (x.reshape(bsz, length, dm), jnp.stack(fin_dn, axis=1), jnp.stack(fin_rw, axis=1),
            jnp.stack(fin_lru, axis=1))


def kernel(x_prompt, x_sample, state_dn, state_rwkv, state_lru, c, c_ctx, w_mod, b_mod, norm_g, ffn_w_up, ffn_conv, ffn_w_down, ev_w_in, ev_w_out, dn_conv, dn_a_log, dn_dt_bias, dn_norm_g, rw_mu, rw_w0, rw_w2, rw_a0, rw_a2, rw_g2, rw_k_k, rw_k_a, rw_r_k, rw_ln_g, rw_ln_b, od_w_in, od_w_out, lru_conv, lru_conv_b, lru_w_gate, lru_b_gate, lru_lambda):
    prm = dict(w_mod=w_mod, b_mod=b_mod, norm_g=norm_g, ffn_w_up=ffn_w_up, ffn_conv=ffn_conv,
               ffn_w_down=ffn_w_down, ev_w_in=ev_w_in, ev_w_out=ev_w_out, dn_conv=dn_conv,
               dn_a_log=dn_a_log, dn_dt_bias=dn_dt_bias, dn_norm_g=dn_norm_g, rw_mu=rw_mu, rw_w0=rw_w0,
               rw_w2=rw_w2, rw_a0=rw_a0, rw_a2=rw_a2, rw_g2=rw_g2, rw_k_k=rw_k_k, rw_k_a=rw_k_a,
               rw_r_k=rw_r_k, rw_ln_g=rw_ln_g, rw_ln_b=rw_ln_b, od_w_in=od_w_in, od_w_out=od_w_out,
               lru_conv=lru_conv, lru_conv_b=lru_conv_b, lru_w_gate=lru_w_gate, lru_b_gate=lru_b_gate,
               lru_lambda=lru_lambda)
    bp = x_prompt.shape[0]
    grid_w = 64
    depth, dm = w_mod.shape[0], w_mod.shape[1]
    cond = jnp.concatenate([c_ctx[None, :], c], axis=0)
    mod = modulation_all(jnp.pad(cond, ((0, (-cond.shape[0]) % 16), (0, 0))), w_mod, b_mod)
    mod = mod.reshape(depth, -1, 6, 1, dm)
    y_prompt, new_dn, new_rw, new_lru = _trunk(
        x_prompt, mod[:, :1], 1,
        jnp.zeros((bp,) + state_dn.shape[1:], F32),
        jnp.zeros((bp,) + state_rwkv.shape[1:], F32),
        jnp.zeros((bp,) + state_lru.shape[1:], F32), prm)
    rows = x_sample.shape[1] // grid_w
    y_sample, _, _, _ = _trunk(x_sample, mod[:, 1:1 + c.shape[0]], rows, state_dn, state_rwkv, state_lru,
                               prm)
    return (y_prompt, y_sample, new_dn, new_rw, new_lru)
```

```python
import functools
import math

import jax
import jax.numpy as jnp
from jax import lax
from jax.experimental import pallas as pl
from jax.experimental.pallas import tpu as pltpu

F32 = jnp.float32
BF16 = jnp.bfloat16
HIGHEST = lax.Precision.HIGHEST

NORM_EPS = 1e-6
CHUNK = 64
VMEM_LIMIT_BYTES = 48 * 1024 * 1024
VMEM_LIMIT_LARGE_BYTES = 56 * 1024 * 1024


def _cparams(*sem, vmem=None):
    return pltpu.CompilerParams(dimension_semantics=sem, vmem_limit_bytes=vmem or VMEM_LIMIT_BYTES)


def _dotb(a, b):
    return jnp.dot(a.astype(BF16), b.astype(BF16), preferred_element_type=F32)


def _dotb_nt(a, b):
    return lax.dot_general(a.astype(BF16), b.astype(BF16), (((1,), (1,)), ((), ())),
                           preferred_element_type=F32)


def _dotb_tn(a, b):
    return lax.dot_general(a.astype(BF16), b.astype(BF16), (((0,), (0,)), ((), ())),
                           preferred_element_type=F32)


def _dot_hi(a, b):
    return jnp.dot(a, b, preferred_element_type=F32, precision=HIGHEST)


def _mm_body(a_ref, b_ref, o_ref):
    o_ref[...] = jnp.dot(a_ref[...], b_ref[...], preferred_element_type=F32).astype(o_ref.dtype)


LANES = 128


def _tile(n, target):
    if n <= target:
        return n
    best = None
    for t in range(LANES, target + 1, LANES):
        if n % t == 0:
            best = t
    assert best is not None, n
    return best


def matmul(a, b, out_dtype=F32, tm=1024, tn=1536):
    m, k = a.shape
    n = b.shape[1]
    tm, tn = _tile(m, tm), _tile(n, tn)
    return pl.pallas_call(
        _mm_body,
        grid=(m // tm, n // tn),
        in_specs=[pl.BlockSpec((tm, k), lambda i, j: (i, 0)),
                  pl.BlockSpec((k, tn), lambda i, j: (0, j))],
        out_specs=pl.BlockSpec((tm, tn), lambda i, j: (i, j)),
        out_shape=jax.ShapeDtypeStruct((m, n), out_dtype),
        compiler_params=_cparams("parallel", "arbitrary"),
    )(a.astype(BF16), b.astype(BF16))


def _mod_body(c_ref, w_ref, b_ref, o_ref):
    o_ref[0] = jnp.dot(c_ref[...], w_ref[0].astype(BF16), preferred_element_type=F32) + b_ref[0]


def modulation_all(cvec, w_mod, b_mod, tn=1024):
    nc, dm = cvec.shape
    depth, _, n = w_mod.shape
    return pl.pallas_call(
        _mod_body,
        grid=(depth, n // tn),
        in_specs=[pl.BlockSpec((nc, dm), lambda l, j: (0, 0)),
                  pl.BlockSpec((1, dm, tn), lambda l, j: (l, 0, j)),
                  pl.BlockSpec((1, 1, tn), lambda l, j: (l, 0, j))],
        out_specs=pl.BlockSpec((1, nc, tn), lambda l, j: (l, 0, j)),
        out_shape=jax.ShapeDtypeStruct((depth, nc, n), F32),
        compiler_params=_cparams("parallel", "parallel"),
    )(jax.nn.silu(cvec).astype(BF16), w_mod, b_mod.reshape(depth, 1, n))


def _norm_mod(x, g, scale, shift):
    return x * lax.rsqrt(jnp.mean(x * x, axis=-1, keepdims=True) + NORM_EPS) * g * (1.0 + scale) + shift


def _mod_spec(which, blocks_per_cond, dm):
    return pl.BlockSpec((1, 1, 1, dm), lambda i, *_: (i // blocks_per_cond, which, 0, 0))


def _norm_mod_body(x_ref, g_ref, sc_ref, sh_ref, h_ref):
    h_ref[...] = _norm_mod(x_ref[...], g_ref[...], sc_ref[0, 0], sh_ref[0, 0]).astype(h_ref.dtype)


def norm_mod(x, g, mod, which_shift, tm=512):
    t, dm = x.shape
    bpc = t // mod.shape[0] // tm
    row = pl.BlockSpec((tm, dm), lambda i: (i, 0))
    return pl.pallas_call(
        _norm_mod_body,
        grid=(t // tm,),
        in_specs=[row, pl.BlockSpec((1, dm), lambda i: (0, 0)),
                  _mod_spec(which_shift + 1, bpc, dm), _mod_spec(which_shift, bpc, dm)],
        out_specs=row,
        out_shape=jax.ShapeDtypeStruct((t, dm), BF16),
        compiler_params=_cparams("parallel"),
    )(x, g.reshape(1, dm), mod, mod)


def _mm_res_body(a_ref, w_ref, x_ref, g_ref, gate_ref, *rest, nk, emit_h):
    if emit_h:
        gn_ref, sc_ref, sh_ref, xo_ref, h_ref = rest
    else:
        (xo_ref,) = rest
    k = pl.program_id(1)

    if nk > 1:
        @pl.when(k == 0)
        def _():
            xo_ref[...] = jnp.zeros_like(xo_ref)

        xo_ref[...] += jnp.dot(a_ref[...], w_ref[...], preferred_element_type=F32)

    @pl.when(k == nk - 1)
    def _():
        y = xo_ref[...] if nk > 1 else jnp.dot(a_ref[...], w_ref[...], preferred_element_type=F32)
        y = y * lax.rsqrt(jnp.mean(y * y, axis=-1, keepdims=True) + NORM_EPS) * g_ref[...]
        xn = x_ref[...] + gate_ref[0, 0] * y
        xo_ref[...] = xn
        if emit_h:
            h_ref[...] = _norm_mod(xn, gn_ref[...], sc_ref[0, 0], sh_ref[0, 0]).astype(h_ref.dtype)


def matmul_residual(a, w, x, g_out, mod, which_gate, next_norm=None, tm=512, tk=2048, vmem=None):
    t, kdim = a.shape
    dm = w.shape[1]
    tk = _tile(kdim, tk)
    nk = kdim // tk
    bpc = t // mod.shape[0] // tm
    row = pl.BlockSpec((tm, dm), lambda i, k: (i, 0))
    vec = pl.BlockSpec((1, dm), lambda i, k: (0, 0))
    in_specs = [pl.BlockSpec((tm, tk), lambda i, k: (i, k)),
                pl.BlockSpec((tk, dm), lambda i, k: (k, 0)),
                row, vec, _mod_spec(which_gate, bpc, dm)]
    args = [a, w, x, g_out.reshape(1, dm), mod]
    out_specs = [row]
    out_shape = [jax.ShapeDtypeStruct((t, dm), F32)]
    if next_norm is not None:
        g_next, mod_next, which_shift = next_norm
        in_specs += [vec, _mod_spec(which_shift + 1, bpc, dm), _mod_spec(which_shift, bpc, dm)]
        args += [g_next.reshape(1, dm), mod_next, mod_next]
        out_specs.append(row)
        out_shape.append(jax.ShapeDtypeStruct((t, dm), BF16))
    out = pl.pallas_call(
        functools.partial(_mm_res_body, nk=nk, emit_h=next_norm is not None),
        grid=(t // tm, nk),
        in_specs=in_specs,
        out_specs=out_specs,
        out_shape=out_shape,
        compiler_params=_cparams("parallel", "arbitrary", vmem=vmem),
    )(*args)
    return out if next_norm is not None else (out[0], None)


def _chunk_iotas(c):
    return lax.broadcasted_iota(jnp.int32, (c, c), 0), lax.broadcasted_iota(jnp.int32, (c, c), 1)


def _unit_tri_inverses(ms, ii, jj, lo, hi, which=None):
    c = ms[0].shape[0]
    eye = (ii == jj).astype(F32)
    if which is None:
        lo, hi, which = [lo], [hi], [0] * len(ms)

    def pair_masks(shift):
        out = []
        for l, h in zip(lo, hi):
            same = (l >> (shift + 1)) == (h >> (shift + 1))
            out.append(same & (((l >> shift) & 1) == 1) & (((h >> shift) & 1) == 0))
        return out

    masks = pair_masks(0)
    ts = [eye - jnp.where(masks[w], m, 0.0) for m, w in zip(ms, which)]
    shift = 1
    while (1 << shift) < c:
        masks = pair_masks(shift)
        tb = [_dotb(t, jnp.where(masks[w], m, 0.0)) for t, m, w in zip(ts, ms, which)]
        ts = [t - _dotb(x, t) for x, t in zip(tb, ts)]
        shift += 1
    return ts


def _delta_body(qf_ref, kf_ref, vf_ref, qb_ref, kb_ref, vb_ref, betaf_ref, betab_ref, gf_ref, gb_ref,
                gtf_ref, gtb_ref, s0_ref, of_ref, ob_ref, s_ref, *, heads, dk, dv):
    c = qf_ref.shape[1]

    @pl.when(pl.program_id(1) == 0)
    def _():
        s_ref[...] = s0_ref[...]

    ii, jj = _chunk_iotas(c)
    los, his = [ii, jj], [jj, ii]
    units = [(d, h) for d in range(2) for h in range(heads)]
    q_refs, k_refs, v_refs = (qf_ref, qb_ref), (kf_ref, kb_ref), (vf_ref, vb_ref)
    gc_col, gc_row, g_tot, beta, incl, strict = [], [], [], [], [], []
    for d, (g_ref, gt_ref, b_ref) in enumerate(((gf_ref, gtf_ref, betaf_ref), (gb_ref, gtb_ref, betab_ref))):
        incl.append(los[d] >= his[d])
        strict.append(los[d] > his[d])
        m_incl = incl[d].astype(F32)
        g = g_ref[0, 0]
        gc_col.append(_dot_hi(m_incl, g))
        gc_row.append(lax.dot_general(gt_ref[0, 0, 0], m_incl, (((1,), (1,)), ((), ())),
                                      preferred_element_type=F32, precision=HIGHEST))
        g_tot.append(jnp.sum(g, axis=0, keepdims=True))
        beta.append(b_ref[0, 0])

    q = [q_refs[d][0, :, h * dk:(h + 1) * dk] for d, h in units]
    k = [k_refs[d][0, :, h * dk:(h + 1) * dk] for d, h in units]
    v = [v_refs[d][0, :, h * dv:(h + 1) * dv] for d, h in units]
    s = [s_ref[0, d, h] for d, h in units]
    gcc = [gc_col[d][:, h:h + 1] for d, h in units]
    bc = [beta[d][:, h:h + 1] for d, h in units]
    gt = [g_tot[d][:, h:h + 1] for d, h in units]
    us = range(len(units))
    dec_incl = []
    for i, (d, h) in enumerate(units):
        diff = gcc[i] - gc_row[d][h:h + 1, :]
        dec_incl.append(jnp.where(incl[d], jnp.exp(jnp.where(incl[d], diff, 0.0)), 0.0))
    kk = [_dotb_nt(k[i], k[i]) for i in us]
    qk = [_dotb_nt(q[i], k[i]) * dec_incl[i] for i in us]
    m = [bc[i] * kk[i] * jnp.where(strict[units[i][0]], dec_incl[i], 0.0) for i in us]
    t = _unit_tri_inverses(m, ii, jj, los, his, which=[d for d, _ in units])
    egc = [jnp.exp(gcc[i]) for i in us]
    sol = [_dotb(t[i], jnp.concatenate([bc[i] * v[i], (bc[i] * egc[i]) * k[i]], axis=1)) for i in us]
    qs = [_dotb(q[i] * egc[i], s[i]) for i in us]
    u = [sol[i][:, :dv] - _dotb(sol[i][:, dv:], s[i]) for i in us]
    o = [qs[i] + _dotb(qk[i], u[i]) for i in us]
    s_new = [jnp.exp(gt[i]) * s[i] + _dotb_tn(k[i] * jnp.exp(gt[i] - gcc[i]), u[i]) for i in us]
    o_refs = (of_ref, ob_ref)
    for i, (d, h) in enumerate(units):
        s_ref[0, d, h] = s_new[i]
        o_refs[d][0, 0, :, h * dv:(h + 1) * dv] = o[i]


def delta_scan(qkv, beta, g, s0):
    bsz, length, _ = qkv.shape
    _, _, heads, dk, dv = s0.shape
    assert dk == dv
    c = CHUNK
    n = length // c
    gt = jnp.swapaxes(g.reshape(bsz, 2, n, c, heads), 3, 4)

    def chunk_of(d, i):
        return i + d * (n - 1 - 2 * i)

    tok = lambda part, d: pl.BlockSpec((1, c, heads * dk), lambda b, i: (b, chunk_of(d, i), part))
    par = lambda d: pl.BlockSpec((1, 1, c, heads), lambda b, i: (b, d, chunk_of(d, i), 0))
    row = lambda d: pl.BlockSpec((1, 1, 1, heads, c), lambda b, i: (b, d, chunk_of(d, i), 0, 0))
    st = pl.BlockSpec((1, 2, heads, dk, dv), lambda b, i: (b, 0, 0, 0, 0))
    out = lambda d: pl.BlockSpec((1, 1, c, heads * dv), lambda b, i: (0, b, chunk_of(d, i), 0))
    return pl.pallas_call(
        functools.partial(_delta_body, heads=heads, dk=dk, dv=dv),
        grid=(bsz, n),
        in_specs=[tok(0, 0), tok(1, 0), tok(2, 0), tok(0, 1), tok(1, 1), tok(2, 1),
                  par(0), par(1), par(0), par(1), row(0), row(1), st],
        out_specs=[out(0), out(1), st],
        out_shape=[jax.ShapeDtypeStruct((1, bsz, length, heads * dv), F32),
                   jax.ShapeDtypeStruct((1, bsz, length, heads * dv), F32),
                   jax.ShapeDtypeStruct(s0.shape, F32)],
        compiler_params=_cparams("parallel", "arbitrary"),
    )(qkv, qkv, qkv, qkv, qkv, qkv, beta, beta, g, g, gt, gt, s0)


def _shift_rows(x, s, rows):
    length = x.shape[0]
    y = pltpu.roll(x, s % length, 0)
    return jnp.where(rows >= s if s > 0 else rows < length + s, y, 0.0)


def _short_conv(x, w_ref, rows):
    return (w_ref[0:1, :] * _shift_rows(x, 2, rows) + w_ref[1:2, :] * _shift_rows(x, 1, rows)
            + w_ref[2:3, :] * x + w_ref[3:4, :] * _shift_rows(x, -1, rows))


def _token_shift(x, mu_ref, rows):
    return (x + mu_ref[0:1, :] * (_shift_rows(x, 1, rows) - x)
            + mu_ref[1:2, :] * (_shift_rows(x, -1, rows) - x))


def _group_ones(width, group):
    i = lax.broadcasted_iota(jnp.int32, (width, width), 0) // group
    j = lax.broadcasted_iota(jnp.int32, (width, width), 1) // group
    return (i == j).astype(BF16)


def _group_sum(x, ones):
    hi = x.astype(BF16)
    lo = (x - hi.astype(F32)).astype(BF16)
    return (jnp.dot(hi, ones, preferred_element_type=F32) + jnp.dot(lo, ones, preferred_element_type=F32))


def _delta_prep_body(p_ref, w_ref, o_ref, *, dk, q_blocks):
    j = pl.program_id(1)
    rows = lax.broadcasted_iota(jnp.int32, p_ref.shape[1:], 0)
    y = jax.nn.silu(_short_conv(p_ref[0], w_ref, rows))
    norm_w = jnp.where(j < q_blocks, dk ** -0.5, jnp.where(j < 2 * q_blocks, 1.0, 0.0))
    plain_w = jnp.where(j < 2 * q_blocks, 0.0, 1.0)
    for h in range(y.shape[1] // dk):
        yh = y[:, h * dk:(h + 1) * dk]
        inv = lax.rsqrt(jnp.sum(yh * yh, axis=-1, keepdims=True) + 1e-6)
        o_ref[0, :, h * dk:(h + 1) * dk] = yh * (inv * norm_w + plain_w)


def delta_prep(p, conv_w, heads, dk, tc=512):
    bsz, length, _ = p.shape
    width = heads * dk
    spec = pl.BlockSpec((1, length, tc), lambda b, j: (b, 0, j))
    return pl.pallas_call(
        functools.partial(_delta_prep_body, dk=dk, q_blocks=width // tc),
        grid=(bsz, 3 * width // tc),
        in_specs=[spec, pl.BlockSpec((4, tc), lambda b, j: (0, j))],
        out_specs=spec,
        out_shape=jax.ShapeDtypeStruct((bsz, length, 3 * width), F32),
        compiler_params=_cparams("parallel", "parallel"),
    )(p, conv_w)


def _rwkv_prep_body(r_ref, k_ref, v_ref, sm_ref, mur_ref, muk_ref, muv_ref, mus_ref,
                    w0_ref, w2_ref, a0_ref, a2_ref, g2_ref, kkw_ref, kaw_ref,
                    ro_ref, vo_ref, kk_ref, lw_ref, kd_ref, ag_ref, gate_ref, *, hn, r_decay, r_a, r_gate):
    rows = lax.broadcasted_iota(jnp.int32, r_ref.shape[1:], 0)
    rows_s = lax.broadcasted_iota(jnp.int32, sm_ref.shape[1:], 0)
    ro_ref[0] = _token_shift(r_ref[0], mur_ref, rows)
    vo_ref[0] = _token_shift(v_ref[0], muv_ref, rows)
    kr = _token_shift(k_ref[0], muk_ref, rows)
    sm = _token_shift(sm_ref[0], mus_ref, rows_s)
    wd = jnp.tanh(sm[:, :2 * r_decay])
    ad = sm[:, 2 * r_decay:2 * r_decay + 2 * r_a]
    gd = jax.nn.sigmoid(sm[:, 2 * r_decay + 2 * r_a:2 * r_decay + 2 * r_a + r_gate])
    gate_ref[0] = _dotb(gd, g2_ref[...])
    kx = kr * kkw_ref[...]
    ones = _group_ones(kx.shape[1], hn)
    kk_ref[0] = kx * lax.rsqrt(_group_sum(kx * kx, ones) + 1e-6)
    for d in range(2):
        w_pre = w0_ref[d:d + 1, :] + _dotb(wd[:, d * r_decay:(d + 1) * r_decay], w2_ref[d])
        lw_ref[0, d] = -jnp.exp(-jax.nn.softplus(-w_pre) - 0.5)
        a_g = jax.nn.sigmoid(a0_ref[d:d + 1, :] + _dotb(ad[:, d * r_a:(d + 1) * r_a], a2_ref[d]))
        ag_ref[0, d] = a_g
        kd_ref[0, d] = kr * (1.0 + (a_g - 1.0) * kaw_ref[...])


def rwkv_prep(p, col0, small_col0, small_w, mu_rkv, mu_small, w0, w2, a0, a2, g2, k_k, k_a, hn, tc=256):
    bsz, length, _ = p.shape
    width = w0.shape[1]
    r_decay, r_a, r_gate = w2.shape[1], a2.shape[1], g2.shape[0]
    nt = width // tc
    assert col0 % tc == 0 and small_col0 % small_w == 0 and width % tc == 0

    def part(i):
        return pl.BlockSpec((1, length, tc), lambda b, j: (b, 0, col0 // tc + i * nt + j))

    def mu_part(i):
        return pl.BlockSpec((2, tc), lambda b, j: (0, i * nt + j))

    vec = pl.BlockSpec((1, tc), lambda b, j: (0, j))
    dvec = pl.BlockSpec((2, tc), lambda b, j: (0, j))
    tok = pl.BlockSpec((1, length, tc), lambda b, j: (b, 0, j))
    dtok = pl.BlockSpec((1, 2, length, tc), lambda b, j: (b, 0, 0, j))
    one = jax.ShapeDtypeStruct((bsz, length, width), F32)
    two = jax.ShapeDtypeStruct((bsz, 2, length, width), F32)
    return pl.pallas_call(
        functools.partial(_rwkv_prep_body, hn=hn, r_decay=r_decay, r_a=r_a, r_gate=r_gate),
        grid=(bsz, nt),
        in_specs=[part(0), part(1), part(2),
                  pl.BlockSpec((1, length, small_w), lambda b, j: (b, 0, small_col0 // small_w)),
                  mu_part(0), mu_part(1), mu_part(2),
                  pl.BlockSpec((2, small_w), lambda b, j: (0, 0)),
                  dvec, pl.BlockSpec((2, r_decay, tc), lambda b, j: (0, 0, j)),
                  dvec, pl.BlockSpec((2, r_a, tc), lambda b, j: (0, 0, j)),
                  pl.BlockSpec((r_gate, tc), lambda b, j: (0, j)), vec, vec],
        out_specs=[tok, tok, tok, dtok, dtok, dtok, tok],
        out_shape=[one, one, one, two, two, two, one],
        compiler_params=_cparams("parallel", "parallel"),
    )(p, p, p, p, mu_rkv, mu_rkv, mu_rkv, mu_small, w0, w2, a0, a2, g2,
      k_k.reshape(1, width), k_a.reshape(1, width))


def _even_post_body(of_ref, ob_ref, z_ref, dg_ref, yf_ref, yb_ref, r_ref, v_ref, kf_ref, kb_ref,
                    gate_ref, rk_ref, lg_ref, lb_ref, out_ref, *, dv, hn, gn_eps):
    o = of_ref[0, 0] + ob_ref[0, 0]
    wa = o.shape[1]
    z = z_ref[0]
    for h in range(wa // dv):
        sl = slice(h * dv, (h + 1) * dv)
        oh = o[:, sl]
        oh = oh * lax.rsqrt(jnp.mean(oh * oh, axis=-1, keepdims=True) + NORM_EPS) * dg_ref[...]
        out_ref[0, :, sl] = (oh * jax.nn.silu(z[:, sl])).astype(out_ref.dtype)
    y = yf_ref[0, 0] + yb_ref[0, 0]
    ones = _group_ones(LANES, hn)
    for t in range(y.shape[1] // LANES):
        sl = slice(t * LANES, (t + 1) * LANES)
        yt = y[:, sl]
        cen = yt - _group_sum(yt, ones) * (1.0 / hn)
        var = _group_sum(cen * cen, ones) * (1.0 / hn)
        yn = cen * lax.rsqrt(var + gn_eps) * lg_ref[:, sl] + lb_ref[:, sl]
        rk = r_ref[0, :, sl] * (kf_ref[0, 0, :, sl] + kb_ref[0, 0, :, sl]) * rk_ref[:, sl]
        bonus = _group_sum(rk, ones) * v_ref[0, :, sl]
        out_ref[0, :, wa + t * LANES:wa + (t + 1) * LANES] = ((yn + bonus) * gate_ref[0, :, sl]).astype(
            out_ref.dtype)


def even_post(o_f, o_b, p, z_col, dn_norm_g, y_f, y_b, r, v, kdir, gate, r_k, ln_g, ln_b, hn, tl=256):
    _, bsz, length, wa = o_f.shape
    lead = lambda w: pl.BlockSpec((1, 1, tl, w), lambda b, i: (0, b, i, 0))
    wb = y_f.shape[3]
    dv = dn_norm_g.shape[0]
    assert z_col % wa == 0
    dirs = lambda d, w: pl.BlockSpec((1, 1, tl, w), lambda b, i: (b, d, i, 0))
    tok = lambda w: pl.BlockSpec((1, tl, w), lambda b, i: (b, i, 0))
    vec = lambda w: pl.BlockSpec((1, w), lambda b, i: (0, 0))
    return pl.pallas_call(
        functools.partial(_even_post_body, dv=dv, hn=hn, gn_eps=hn * 1e-5),
        grid=(bsz, length // tl),
        in_specs=[lead(wa), lead(wa),
                  pl.BlockSpec((1, tl, wa), lambda b, i: (b, i, z_col // wa)), vec(dv),
                  lead(wb), lead(wb), tok(wb), tok(wb), dirs(0, wb), dirs(1, wb), tok(wb),
                  vec(wb), vec(wb), vec(wb)],
        out_specs=tok(wa + wb),
        out_shape=jax.ShapeDtypeStruct((bsz, length, wa + wb), BF16),
        compiler_params=_cparams("parallel", "parallel"),
    )(o_f, o_b, p, dn_norm_g.reshape(1, dv), y_f, y_b, r, v, kdir, kdir, gate,
      r_k.reshape(1, wb), ln_g.reshape(1, wb), ln_b.reshape(1, wb))


def _rwkv_body(rf_ref, vf_ref, kkf_ref, rb_ref, vb_ref, kkb_ref, lwf_ref, lwb_ref, kdf_ref, kdb_ref,
               agf_ref, agb_ref, s0_ref, yf_ref, yb_ref, s_ref, *, heads, hn):
    c = rf_ref.shape[1]

    @pl.when(pl.program_id(1) == 0)
    def _():
        s_ref[...] = s0_ref[...]

    ii, jj = _chunk_iotas(c)
    los, his = [ii, jj], [jj, ii]
    i2 = lax.broadcasted_iota(jnp.int32, (c, 2 * c), 0)
    j2 = lax.broadcasted_iota(jnp.int32, (c, 2 * c), 1) & (c - 1)
    incl2 = [i2 >= j2, j2 >= i2]
    strict = [ii > jj, jj > ii]
    a_t, r_t, b_t, k_t, b_p, k_p, p_c, v = [], [], [], [], [], [], [], []
    dir_refs = ((rf_ref, vf_ref, kkf_ref, lwf_ref, kdf_ref, agf_ref),
                (rb_ref, vb_ref, kkb_ref, lwb_ref, kdb_ref, agb_ref))
    for d, (r_ref, v_ref, kk_ref, lw_ref, kd_ref, ag_ref) in enumerate(dir_refs):
        logw = lw_ref[0, 0]
        lw = _dot_hi((los[d] >= his[d]).astype(F32), logw)
        lw_tot = jnp.sum(logw, axis=0, keepdims=True)
        kk = kk_ref[0]
        kd = kd_ref[0, 0]
        bb = kk * ag_ref[0, 0]
        e_out = jnp.exp(-lw)
        e_rem = jnp.exp(lw_tot - lw)
        a_t.append(-kk * jnp.exp(lw - logw))
        r_t.append(r_ref[0] * jnp.exp(lw))
        b_t.append(bb * e_out)
        k_t.append(kd * e_out)
        b_p.append(bb * e_rem)
        k_p.append(kd * e_rem)
        p_c.append(jnp.exp(lw_tot))
        v.append(v_ref[0])

    units = [(d, h) for d in range(2) for h in range(heads)]
    us = range(len(units))
    sl = [slice(h * hn, (h + 1) * hn) for _, h in units]
    dr = [d for d, _ in units]
    s = [s_ref[0, d, h] for d, h in units]
    lhs = [jnp.concatenate([a_t[dr[i]][:, sl[i]], r_t[dr[i]][:, sl[i]]], axis=0) for i in us]
    rhs = [jnp.concatenate([b_t[dr[i]][:, sl[i]], k_t[dr[i]][:, sl[i]]], axis=0) for i in us]
    gram = [_dotb_nt(lhs[i], rhs[i]) for i in us]
    ls = [_dotb_nt(lhs[i], s[i]) for i in us]
    t = _unit_tri_inverses([jnp.where(strict[dr[i]], -gram[i][:c, :c], 0.0) for i in us],
                           ii, jj, los, his, which=dr)
    vh = [v[dr[i]][:, sl[i]] for i in us]
    pre = [ls[i][:c] + _dotb(jnp.where(strict[dr[i]], gram[i][:c, c:], 0.0), vh[i]) for i in us]
    uv = [jnp.concatenate([_dotb(t[i], pre[i]), vh[i]], axis=0) for i in us]
    y = [ls[i][c:] + _dotb(jnp.where(incl2[dr[i]], gram[i][c:], 0.0), uv[i]) for i in us]
    s_new = [s[i] * p_c[dr[i]][:, sl[i]]
             + _dotb_tn(uv[i], jnp.concatenate([b_p[dr[i]][:, sl[i]], k_p[dr[i]][:, sl[i]]], axis=0))
             for i in us]
    y_refs = (yf_ref, yb_ref)
    for i, (d, h) in enumerate(units):
        s_ref[0, d, h] = s_new[i]
        y_refs[d][0, 0, :, sl[i]] = y[i]


def rwkv_scan(r, v, kk, logw, kdir, ag, s0):
    bsz, length, width = r.shape
    _, _, heads, hn, _ = s0.shape
    c = CHUNK
    n = length // c

    def chunk_of(d, i):
        return i + d * (n - 1 - 2 * i)

    tok = lambda d: pl.BlockSpec((1, c, width), lambda b, i: (b, chunk_of(d, i), 0))
    par = lambda d: pl.BlockSpec((1, 1, c, width), lambda b, i: (b, d, chunk_of(d, i), 0))
    out = lambda d: pl.BlockSpec((1, 1, c, width), lambda b, i: (0, b, chunk_of(d, i), 0))
    st = pl.BlockSpec((1, 2, heads, hn, hn), lambda b, i: (b, 0, 0, 0, 0))
    one_dir = jax.ShapeDtypeStruct((1, bsz, length, width), F32)
    return pl.pallas_call(
        functools.partial(_rwkv_body, heads=heads, hn=hn),
        grid=(bsz, n),
        in_specs=[tok(0), tok(0), tok(0), tok(1), tok(1), tok(1),
                  par(0), par(1), par(0), par(1), par(0), par(1), st],
        out_specs=[out(0), out(1), st],
        out_shape=[one_dir, one_dir, jax.ShapeDtypeStruct(s0.shape, F32)],
        compiler_params=_cparams("parallel", "arbitrary"),
    )(r, v, kk, r, v, kk, logw, logw, kdir, kdir, ag, ag, s0)


LRU_C = 8.0


def _scan_pass(a, b, shift, rows, rev):
    length = a.shape[0]
    if rev:
        a_s = pltpu.roll(a, length - shift, 0)
        b_s = pltpu.roll(b, length - shift, 0)
        valid = rows < length - shift
    else:
        a_s = pltpu.roll(a, shift, 0)
        b_s = pltpu.roll(b, shift, 0)
        valid = rows >= shift
    a_s = jnp.where(valid, a_s, 1.0)
    b_s = jnp.where(valid, b_s, 0.0)
    return a * a_s, a * b_s + b


SUBLANES = 8


def _blocked_scan(a, b, rows, rev, a_ref, b_ref):
    length, width = a.shape
    sub = rows & (SUBLANES - 1)
    for shift in (1, 2, 4):
        if rev:
            a_s, b_s = pltpu.roll(a, length - shift, 0), pltpu.roll(b, length - shift, 0)
            valid = sub < SUBLANES - shift
        else:
            a_s, b_s = pltpu.roll(a, shift, 0), pltpu.roll(b, shift, 0)
            valid = sub >= shift
        a, b = a * jnp.where(valid, a_s, 1.0), a * jnp.where(valid, b_s, 0.0) + b
    a_ref[...] = a
    b_ref[...] = b
    nb = length // SUBLANES
    last = 0 if rev else SUBLANES - 1
    a_blk = a_ref[pl.ds(last, nb, stride=SUBLANES), :]
    b_blk = b_ref[pl.ds(last, nb, stride=SUBLANES), :]
    brow = lax.broadcasted_iota(jnp.int32, (nb, width), 0)
    shift = 1
    while shift < nb:
        a_blk, b_blk = _scan_pass(a_blk, b_blk, shift, brow, rev)
        shift *= 2
    carry = _shift_rows(b_blk, -1 if rev else 1, brow)
    carry = jnp.broadcast_to(carry[:, None, :], (nb, SUBLANES, width)).reshape(length, width)
    return a * carry + b


def _lru_body(x_ref, gbr_ref, cw_ref, cb_ref, wg_ref, bg_ref, lam_ref, s0_ref, o_ref, s_ref, a_scr, b_scr):
    length, bs = a_scr.shape
    rows = lax.broadcasted_iota(jnp.int32, (length, bs), 0)
    for blk in range(x_ref.shape[2] // bs):
        sl = slice(blk * bs, (blk + 1) * bs)
        x = _short_conv(x_ref[0, :, sl], cw_ref[:, sl], rows) + cb_ref[:, sl]
        xb = x.astype(BF16)
        y = jnp.zeros_like(x)
        for d in range(2):
            rev = d == 1
            r_g = jax.nn.sigmoid(jnp.dot(xb, wg_ref[d, 0, blk].astype(BF16), preferred_element_type=F32)
                                 + bg_ref[d, 0:1, sl])
            i_g = jax.nn.sigmoid(jnp.dot(xb, wg_ref[d, 1, blk].astype(BF16), preferred_element_type=F32)
                                 + bg_ref[d, 1:2, sl])
            log_a = -LRU_C * r_g * jax.nn.softplus(-lam_ref[d:d + 1, sl])
            a = jnp.exp(log_a)
            b = jnp.sqrt(-jnp.tanh(log_a) * (a * a + 1.0)) * (i_g * x)
            first = (rows == (length - 1 if rev else 0))
            b = jnp.where(first, b + a * s0_ref[0, d:d + 1, sl], b)
            b = _blocked_scan(a, b, rows, rev, a_scr, b_scr)
            y = y + b
            s_ref[0, d:d + 1, sl] = b[0:1, :] if rev else b[length - 1:length, :]
        o_ref[0, :, sl] = (jax.nn.gelu(gbr_ref[0, :, sl]) * y).astype(o_ref.dtype)


def lru_mix(p, conv_w, conv_b, w_gate, b_gate, lam, s0, blocks_per_step=2):
    bsz, length, _ = p.shape
    nb, bs = w_gate.shape[2], w_gate.shape[3]
    width = nb * bs
    bps = blocks_per_step
    assert nb % bps == 0
    wide = bs * bps
    tok = pl.BlockSpec((1, length, wide), lambda b, j: (b, 0, j))
    st = pl.BlockSpec((1, 2, wide), lambda b, j: (b, 0, j))
    return pl.pallas_call(
        _lru_body,
        grid=(bsz, nb // bps),
        in_specs=[pl.BlockSpec((1, length, wide), lambda b, j: (b, 0, nb // bps + j)), tok,
                  pl.BlockSpec((4, wide), lambda b, j: (0, j)),
                  pl.BlockSpec((1, wide), lambda b, j: (0, j)),
                  pl.BlockSpec((2, 2, bps, bs, bs), lambda b, j: (0, 0, j, 0, 0)),
                  pl.BlockSpec((2, 2, wide), lambda b, j: (0, 0, j)),
                  pl.BlockSpec((2, wide), lambda b, j: (0, j)),
                  st],
        out_specs=[tok, st],
        out_shape=[jax.ShapeDtypeStruct((bsz, length, width), BF16),
                   jax.ShapeDtypeStruct((bsz, 2, width), F32)],
        scratch_shapes=[pltpu.VMEM((length, bs), F32)] * 2,
        compiler_params=_cparams("parallel", "parallel"),
    )(p, p, conv_w, conv_b.reshape(1, width), w_gate, b_gate, lam, s0)


def _dwconv(x, w9, cols, grid_w, image_rows):
    tokens = x.shape[0]
    w = lambda i, j: w9[3 * i + j:3 * i + j + 1, :]
    x_l = pltpu.roll(jnp.where(cols != grid_w - 1, x, 0.0), 1, 0)
    x_r = pltpu.roll(jnp.where(cols != 0, x, 0.0), tokens - 1, 0)
    out = w(1, 0) * x_l + w(1, 1) * x + w(1, 2) * x_r
    if image_rows > 1:
        assert tokens == grid_w * image_rows
        up = w(0, 0) * x_l + w(0, 1) * x + w(0, 2) * x_r
        dn = w(2, 0) * x_l + w(2, 1) * x + w(2, 2) * x_r
        zero = jnp.zeros((grid_w, x.shape[1]), x.dtype)
        out = out + jnp.concatenate([zero, up[:tokens - grid_w]], axis=0)
        out = out + jnp.concatenate([dn[grid_w:], zero], axis=0)
    return out


FFN_ROW_CHUNK = 256


def _ffn_up_body(h_ref, wa_ref, wb_ref, ca_ref, cb_ref, o_ref, *, grid_w, image_rows):
    cols = lax.broadcasted_iota(jnp.int32, o_ref.shape, 0) & (grid_w - 1)
    chunks = [h_ref[r:r + FFN_ROW_CHUNK, :] for r in range(0, h_ref.shape[0], FFN_ROW_CHUNK)]
    ua = jnp.concatenate([jnp.dot(hc, wa_ref[...], preferred_element_type=F32) for hc in chunks], axis=0)
    ub = jnp.concatenate([jnp.dot(hc, wb_ref[...], preferred_element_type=F32) for hc in chunks], axis=0)
    ca = _dwconv(ua, ca_ref[...], cols, grid_w, image_rows)
    cb = _dwconv(ub, cb_ref[...], cols, grid_w, image_rows)
    o_ref[...] = (jax.nn.silu(ca) * cb).astype(o_ref.dtype)


def ffn_up_conv(h, w_up, w_conv, grid_w, image_rows, tb=1024, tc=512):
    t, dm = h.shape
    f = w_up.shape[1] // 2
    nf = f // tc
    assert t % tb == 0 and f % tc == 0 and tb % (grid_w * image_rows) == 0
    assert image_rows == 1 or tb == grid_w * image_rows
    return pl.pallas_call(
        functools.partial(_ffn_up_body, grid_w=grid_w, image_rows=image_rows),
        grid=(t // tb, nf),
        in_specs=[pl.BlockSpec((tb, dm), lambda i, j: (i, 0)),
                  pl.BlockSpec((dm, tc), lambda i, j: (0, j)),
                  pl.BlockSpec((dm, tc), lambda i, j: (0, j + nf)),
                  pl.BlockSpec((9, tc), lambda i, j: (0, j)),
                  pl.BlockSpec((9, tc), lambda i, j: (0, j + nf))],
        out_specs=pl.BlockSpec((tb, tc), lambda i, j: (i, j)),
        out_shape=jax.ShapeDtypeStruct((t, f), BF16),
        compiler_params=_cparams("parallel", "arbitrary"),
    )(h, w_up, w_up, w_conv, w_conv)


def _reorder_in_proj(w_in, dn_w, heads2, rw_c, small_w):
    a_end = dn_w + 2 * heads2
    small = jnp.concatenate([w_in[:, a_end + 3 * rw_c:], w_in[:, dn_w:a_end]], axis=1)
    small = jnp.pad(small, ((0, 0), (0, small_w - small.shape[1])))
    return jnp.concatenate([w_in[:, :dn_w], w_in[:, a_end:a_end + 3 * rw_c], small], axis=1)


def _even_mixer(h, w_in, dn_conv, dn_a_log, dn_dt_bias, dn_norm_g,
                rw_mu, rw_w0, rw_w2, rw_a0, rw_a2, rw_g2, rw_k_k, rw_k_a, rw_r_k, rw_ln_g, rw_ln_b,
                s_dn, s_rw):
    bsz, length, dm = h.shape
    dn_heads, dn_dk, dn_dv = s_dn.shape[2:]
    rw_heads, rw_n = s_rw.shape[2:4]
    dn_qk, rw_c = dn_heads * dn_dk, rw_heads * rw_n
    dn_w = 3 * dn_qk + dn_heads * dn_dv
    n_small = w_in.shape[1] - dn_w - 4 * dn_heads - 3 * rw_c
    small_w = 512
    small0 = dn_w + 3 * rw_c
    w_r = _reorder_in_proj(w_in, dn_w, 2 * dn_heads, rw_c, small_w)
    p = matmul(h.reshape(bsz * length, dm), w_r).reshape(bsz, length, -1)
    b_raw = p[..., small0 + n_small:small0 + n_small + 2 * dn_heads]
    a_raw = p[..., small0 + n_small + 2 * dn_heads:small0 + n_small + 4 * dn_heads]
    beta = jax.nn.sigmoid(b_raw).reshape(bsz, length, 2, dn_heads)
    g = -jnp.exp(dn_a_log) * jax.nn.softplus(a_raw.reshape(bsz, length, 2, dn_heads) + dn_dt_bias)
    qkv = delta_prep(p, dn_conv, dn_heads, dn_dk)
    o_f, o_b, s_dn_fin = delta_scan(qkv, jnp.swapaxes(beta, 1, 2), jnp.swapaxes(g, 1, 2), s_dn)
    mu_small = jnp.pad(rw_mu[:, 3 * rw_c:], ((0, 0), (0, small_w - n_small)))
    r, vr, kk, logw, kdir, a_g, gate = rwkv_prep(
        p, dn_w, small0, small_w, rw_mu[:, :3 * rw_c], mu_small, rw_w0, rw_w2, rw_a0, rw_a2, rw_g2,
        rw_k_k, rw_k_a, rw_n)
    y_f, y_b, s_rw_fin = rwkv_scan(r, vr, kk, logw, kdir, a_g, s_rw)
    cat = even_post(o_f, o_b, p, 3 * dn_qk, dn_norm_g, y_f, y_b, r, vr, kdir, gate, rw_r_k.reshape(-1), rw_ln_g, rw_ln_b,
                    rw_n)
    return cat.reshape(bsz * length, -1), s_dn_fin, s_rw_fin


def _odd_mixer(h, w_in, conv_w, conv_b, w_gate, b_gate, lam, s_lru):
    bsz, length, dm = h.shape
    p = matmul(h.reshape(bsz * length, dm), w_in).reshape(bsz, length, -1)
    a, s_fin = lru_mix(p, conv_w, conv_b, w_gate, b_gate, lam, s_lru)
    return a.reshape(bsz * length, -1), s_fin


def _trunk(x, mod, rows, s_dn, s_rw, s_lru, prm):
    depth = mod.shape[0]
    bsz, length, dm = x.shape
    fin_dn, fin_rw, fin_lru = [], [], []
    x = x.reshape(bsz * length, dm)
    h = norm_mod(x, prm["norm_g"][0, 0], mod[0], 0)
    for l in range(depth):
        g = prm["norm_g"][l]
        h = h.reshape(bsz, length, dm)
        i = l // 2
        if l % 2 == 0:
            mix, sd, sr = _even_mixer(
                h, prm["ev_w_in"][i], prm["dn_conv"][i], prm["dn_a_log"][i],
                prm["dn_dt_bias"][i], prm["dn_norm_g"][i], prm["rw_mu"][i], prm["rw_w0"][i],
                prm["rw_w2"][i], prm["rw_a0"][i], prm["rw_a2"][i], prm["rw_g2"][i], prm["rw_k_k"][i],
                prm["rw_k_a"][i], prm["rw_r_k"][i], prm["rw_ln_g"][i], prm["rw_ln_b"][i],
                s_dn[:, i], s_rw[:, i])
            fin_dn.append(sd)
            fin_rw.append(sr)
            w_out = prm["ev_w_out"][i]
        else:
            mix, sl = _odd_mixer(h, prm["od_w_in"][i], prm["lru_conv"][i],
                                 prm["lru_conv_b"][i], prm["lru_w_gate"][i], prm["lru_b_gate"][i],
                                 prm["lru_lambda"][i], s_lru[:, i])
            fin_lru.append(sl)
            w_out = prm["od_w_out"][i]
        x, h = matmul_residual(mix, w_out.astype(BF16), x, g[1], mod[l], 2, next_norm=(g[2], mod[l], 3))
        mid = ffn_up_conv(h, prm["ffn_w_up"][l].astype(BF16), prm["ffn_conv"][l].reshape(9, -1),
                          length // rows, rows)
        nxt = (prm["norm_g"][l + 1, 0], mod[l + 1], 0) if l + 1 < depth else None
        x, h = matmul_residual(mid, prm["ffn_w_down"][l].astype(BF16), x, g[3], mod[l], 5, next_norm=nxt,
                               tm=1024, tk=512, vmem=VMEM_LIMIT_LARGE_BYTES)
    return (x.reshape(bsz, length, dm), jnp.stack(fin_dn, axis=1), jnp.stack(fin_rw, axis=1),
            jnp.stack(fin_lru, axis=1))


def kernel(x_prompt, x_sample, state_dn, state_rwkv, state_lru, c, c_ctx, w_mod, b_mod, norm_g, ffn_w_up, ffn_conv, ffn_w_down, ev_w_in, ev_w_out, dn_conv, dn_a_log, dn_dt_bias, dn_norm_g, rw_mu, rw_w0, rw_w2, rw_a0, rw_a2, rw_g2, rw_k_k, rw_k_a, rw_r_k, rw_ln_g, rw_ln_b, od_w_in, od_w_out, lru_conv, lru_conv_b, lru_w_gate, lru_b_gate, lru_lambda):
    prm = dict(w_mod=w_mod, b_mod=b_mod, norm_g=norm_g, ffn_w_up=ffn_w_up, ffn_conv=ffn_conv,
               ffn_w_down=ffn_w_down, ev_w_in=ev_w_in, ev_w_out=ev_w_out, dn_conv=dn_conv,
               dn_a_log=dn_a_log, dn_dt_bias=dn_dt_bias, dn_norm_g=dn_norm_g, rw_mu=rw_mu, rw_w0=rw_w0,
               rw_w2=rw_w2, rw_a0=rw_a0, rw_a2=rw_a2, rw_g2=rw_g2, rw_k_k=rw_k_k, rw_k_a=rw_k_a,
               rw_r_k=rw_r_k, rw_ln_g=rw_ln_g, rw_ln_b=rw_ln_b, od_w_in=od_w_in, od_w_out=od_w_out,
               lru_conv=lru_conv, lru_conv_b=lru_conv_b, lru_w_gate=lru_w_gate, lru_b_gate=lru_b_gate,
               lru_lambda=lru_lambda)
    bp = x_prompt.shape[0]
    grid_w = 64
    depth, dm = w_mod.shape[0], w_mod.shape[1]
    cond = jnp.concatenate([c_ctx[None, :], c], axis=0)
    mod = modulation_all(jnp.pad(cond, ((0, (-cond.shape[0]) % 16), (0, 0))), w_mod, b_mod)
    mod = mod.reshape(depth, -1, 6, 1, dm)
    y_prompt, new_dn, new_rw, new_lru = _trunk(
        x_prompt, mod[:, :1], 1,
        jnp.zeros((bp,) + state_dn.shape[1:], F32),
        jnp.zeros((bp,) + state_rwkv.shape[1:], F32),
        jnp.zeros((bp,) + state_lru.shape[1:], F32), prm)
    rows = x_sample.shape[1] // grid_w
    y_sample, _, _, _ = _trunk(x_sample, mod[:, 1:1 + c.shape[0]], rows, state_dn, state_rwkv, state_lru,
                               prm)
    return (y_prompt, y_sample, new_dn, new_rw, new_lru)
```

```python
import functools
import math

import jax
import jax.numpy as jnp
from jax import lax
from jax.experimental import pallas as pl
from jax.experimental.pallas import tpu as pltpu

F32 = jnp.float32
BF16 = jnp.bfloat16
HIGHEST = lax.Precision.HIGHEST

NORM_EPS = 1e-6
CHUNK = 64
VMEM_LIMIT_BYTES = 48 * 1024 * 1024
VMEM_LIMIT_LARGE_BYTES = 56 * 1024 * 1024


def _cparams(*sem, vmem=None):
    return pltpu.CompilerParams(dimension_semantics=sem, vmem_limit_bytes=vmem or VMEM_LIMIT_BYTES)


def _dotb(a, b):
    return jnp.dot(a.astype(BF16), b.astype(BF16), preferred_element_type=F32)


def _dotb_nt(a, b):
    return lax.dot_general(a.astype(BF16), b.astype(BF16), (((1,), (1,)), ((), ())),
                           preferred_element_type=F32)


def _dotb_tn(a, b):
    return lax.dot_general(a.astype(BF16), b.astype(BF16), (((0,), (0,)), ((), ())),
                           preferred_element_type=F32)


def _dot_hi(a, b):
    return jnp.dot(a, b, preferred_element_type=F32, precision=HIGHEST)


def _mm_body(a_ref, b_ref, o_ref):
    o_ref[...] = jnp.dot(a_ref[...], b_ref[...], preferred_element_type=F32).astype(o_ref.dtype)


LANES = 128


def _tile(n, target):
    if n <= target:
        return n
    best = None
    for t in range(LANES, target + 1, LANES):
        if n % t == 0:
            best = t
    assert best is not None, n
    return best


def matmul(a, b, out_dtype=F32, tm=1024, tn=1536):
    m, k = a.shape
    n = b.shape[1]
    tm, tn = _tile(m, tm), _tile(n, tn)
    return pl.pallas_call(
        _mm_body,
        grid=(m // tm, n // tn),
        in_specs=[pl.BlockSpec((tm, k), lambda i, j: (i, 0)),
                  pl.BlockSpec((k, tn), lambda i, j: (0, j))],
        out_specs=pl.BlockSpec((tm, tn), lambda i, j: (i, j)),
        out_shape=jax.ShapeDtypeStruct((m, n), out_dtype),
        compiler_params=_cparams("parallel", "arbitrary"),
    )(a.astype(BF16), b.astype(BF16))


def _mod_body(c_ref, w_ref, b_ref, o_ref):
    o_ref[0] = jnp.dot(c_ref[...], w_ref[0].astype(BF16), preferred_element_type=F32) + b_ref[0]


def modulation_all(cvec, w_mod, b_mod, tn=1024):
    nc, dm = cvec.shape
    depth, _, n = w_mod.shape
    return pl.pallas_call(
        _mod_body,
        grid=(depth, n // tn),
        in_specs=[pl.BlockSpec((nc, dm), lambda l, j: (0, 0)),
                  pl.BlockSpec((1, dm, tn), lambda l, j: (l, 0, j)),
                  pl.BlockSpec((1, 1, tn), lambda l, j: (l, 0, j))],
        out_specs=pl.BlockSpec((1, nc, tn), lambda l, j: (l, 0, j)),
        out_shape=jax.ShapeDtypeStruct((depth, nc, n), F32),
        compiler_params=_cparams("parallel", "parallel"),
    )(jax.nn.silu(cvec).astype(BF16), w_mod, b_mod.reshape(depth, 1, n))


def _norm_mod(x, g, scale, shift):
    return x * lax.rsqrt(jnp.mean(x * x, axis=-1, keepdims=True) + NORM_EPS) * g * (1.0 + scale) + shift


def _mod_spec(which, blocks_per_cond, dm):
    return pl.BlockSpec((1, 1, 1, dm), lambda i, *_: (i // blocks_per_cond, which, 0, 0))


def _norm_mod_body(x_ref, g_ref, sc_ref, sh_ref, h_ref):
    h_ref[...] = _norm_mod(x_ref[...], g_ref[...], sc_ref[0, 0], sh_ref[0, 0]).astype(h_ref.dtype)


def norm_mod(x, g, mod, which_shift, tm=512):
    t, dm = x.shape
    bpc = t // mod.shape[0] // tm
    row = pl.BlockSpec((tm, dm), lambda i: (i, 0))
    return pl.pallas_call(
        _norm_mod_body,
        grid=(t // tm,),
        in_specs=[row, pl.BlockSpec((1, dm), lambda i: (0, 0)),
                  _mod_spec(which_shift + 1, bpc, dm), _mod_spec(which_shift, bpc, dm)],
        out_specs=row,
        out_shape=jax.ShapeDtypeStruct((t, dm), BF16),
        compiler_params=_cparams("parallel"),
    )(x, g.reshape(1, dm), mod, mod)


def _mm_res_body(a_ref, w_ref, x_ref, g_ref, gate_ref, *rest, nk, emit_h, tail_chunk):
    if emit_h:
        gn_ref, sc_ref, sh_ref, xo_ref, h_ref = rest
    else:
        (xo_ref,) = rest
    k = pl.program_id(1)

    if nk > 1:
        @pl.when(k == 0)
        def _():
            xo_ref[...] = jnp.zeros_like(xo_ref)

        @pl.when(k < nk - 1)
        def _():
            xo_ref[...] += jnp.dot(a_ref[...], w_ref[...], preferred_element_type=F32)

    @pl.when(k == nk - 1)
    def _():
        chunk = tail_chunk or a_ref.shape[0]
        for r in range(0, a_ref.shape[0], chunk):
            rows = slice(r, r + chunk)
            y = jnp.dot(a_ref[rows, :], w_ref[...], preferred_element_type=F32)
            if nk > 1:
                y = y + xo_ref[rows, :]
            y = y * lax.rsqrt(jnp.mean(y * y, axis=-1, keepdims=True) + NORM_EPS) * g_ref[...]
            xn = x_ref[rows, :] + gate_ref[0, 0] * y
            xo_ref[rows, :] = xn
            if emit_h:
                h_ref[rows, :] = _norm_mod(xn, gn_ref[...], sc_ref[0, 0], sh_ref[0, 0]).astype(h_ref.dtype)


def matmul_residual(a, w, x, g_out, mod, which_gate, next_norm=None, tm=512, tk=2048, vmem=None,
                    tail_chunk=None):
    t, kdim = a.shape
    dm = w.shape[1]
    tk = _tile(kdim, tk)
    nk = kdim // tk
    bpc = t // mod.shape[0] // tm
    row = pl.BlockSpec((tm, dm), lambda i, k: (i, 0))
    vec = pl.BlockSpec((1, dm), lambda i, k: (0, 0))
    in_specs = [pl.BlockSpec((tm, tk), lambda i, k: (i, k)),
                pl.BlockSpec((tk, dm), lambda i, k: (k, 0)),
                row, vec, _mod_spec(which_gate, bpc, dm)]
    args = [a, w, x, g_out.reshape(1, dm), mod]
    out_specs = [row]
    out_shape = [jax.ShapeDtypeStruct((t, dm), F32)]
    if next_norm is not None:
        g_next, mod_next, which_shift = next_norm
        in_specs += [vec, _mod_spec(which_shift + 1, bpc, dm), _mod_spec(which_shift, bpc, dm)]
        args += [g_next.reshape(1, dm), mod_next, mod_next]
        out_specs.append(row)
        out_shape.append(jax.ShapeDtypeStruct((t, dm), BF16))
    out = pl.pallas_call(
        functools.partial(_mm_res_body, nk=nk, emit_h=next_norm is not None, tail_chunk=tail_chunk),
        grid=(t // tm, nk),
        in_specs=in_specs,
        out_specs=out_specs,
        out_shape=out_shape,
        compiler_params=_cparams("parallel", "arbitrary", vmem=vmem),
    )(*args)
    return out if next_norm is not None else (out[0], None)


def _chunk_iotas(c):
    return lax.broadcasted_iota(jnp.int32, (c, c), 0), lax.broadcasted_iota(jnp.int32, (c, c), 1)


def _unit_tri_inverses(ms, ii, jj, lo, hi, which=None):
    c = ms[0].shape[0]
    eye = (ii == jj).astype(F32)
    if which is None:
        lo, hi, which = [lo], [hi], [0] * len(ms)

    def pair_masks(shift):
        out = []
        for l, h in zip(lo, hi):
            same = (l >> (shift + 1)) == (h >> (shift + 1))
            out.append(same & (((l >> shift) & 1) == 1) & (((h >> shift) & 1) == 0))
        return out

    masks = pair_masks(0)
    ts = [eye - jnp.where(masks[w], m, 0.0) for m, w in zip(ms, which)]
    shift = 1
    while (1 << shift) < c:
        masks = pair_masks(shift)
        tb = [_dotb(t, jnp.where(masks[w], m, 0.0)) for t, m, w in zip(ts, ms, which)]
        ts = [t - _dotb(x, t) for x, t in zip(tb, ts)]
        shift += 1
    return ts


def _delta_body(qf_ref, kf_ref, vf_ref, qb_ref, kb_ref, vb_ref, betaf_ref, betab_ref, gf_ref, gb_ref,
                gtf_ref, gtb_ref, s0_ref, of_ref, ob_ref, s_ref, *, heads, dk, dv):
    c = qf_ref.shape[1]

    @pl.when(pl.program_id(1) == 0)
    def _():
        s_ref[...] = s0_ref[...]

    ii, jj = _chunk_iotas(c)
    los, his = [ii, jj], [jj, ii]
    units = [(d, h) for d in range(2) for h in range(heads)]
    q_refs, k_refs, v_refs = (qf_ref, qb_ref), (kf_ref, kb_ref), (vf_ref, vb_ref)
    gc_col, gc_row, g_tot, beta, incl, strict = [], [], [], [], [], []
    for d, (g_ref, gt_ref, b_ref) in enumerate(((gf_ref, gtf_ref, betaf_ref), (gb_ref, gtb_ref, betab_ref))):
        incl.append(los[d] >= his[d])
        strict.append(los[d] > his[d])
        m_incl = incl[d].astype(F32)
        g = g_ref[0, 0]
        gc_col.append(_dot_hi(m_incl, g))
        gc_row.append(lax.dot_general(gt_ref[0, 0, 0], m_incl, (((1,), (1,)), ((), ())),
                                      preferred_element_type=F32, precision=HIGHEST))
        g_tot.append(jnp.sum(g, axis=0, keepdims=True))
        beta.append(b_ref[0, 0])

    q = [q_refs[d][0, :, h * dk:(h + 1) * dk] for d, h in units]
    k = [k_refs[d][0, :, h * dk:(h + 1) * dk] for d, h in units]
    v = [v_refs[d][0, :, h * dv:(h + 1) * dv] for d, h in units]
    s = [s_ref[0, d, h] for d, h in units]
    gcc = [gc_col[d][:, h:h + 1] for d, h in units]
    bc = [beta[d][:, h:h + 1] for d, h in units]
    gt = [g_tot[d][:, h:h + 1] for d, h in units]
    us = range(len(units))
    dec_incl = []
    for i, (d, h) in enumerate(units):
        diff = gcc[i] - gc_row[d][h:h + 1, :]
        dec_incl.append(jnp.where(incl[d], jnp.exp(jnp.where(incl[d], diff, 0.0)), 0.0))
    kk = [_dotb_nt(k[i], k[i]) for i in us]
    qk = [_dotb_nt(q[i], k[i]) * dec_incl[i] for i in us]
    m = [bc[i] * kk[i] * jnp.where(strict[units[i][0]], dec_incl[i], 0.0) for i in us]
    t = _unit_tri_inverses(m, ii, jj, los, his, which=[d for d, _ in units])
    egc = [jnp.exp(gcc[i]) for i in us]
    sol = [_dotb(t[i], jnp.concatenate([bc[i] * v[i], (bc[i] * egc[i]) * k[i]], axis=1)) for i in us]
    qs = [_dotb(q[i] * egc[i], s[i]) for i in us]
    u = [sol[i][:, :dv] - _dotb(sol[i][:, dv:], s[i]) for i in us]
    o = [qs[i] + _dotb(qk[i], u[i]) for i in us]
    s_new = [jnp.exp(gt[i]) * s[i] + _dotb_tn(k[i] * jnp.exp(gt[i] - gcc[i]), u[i]) for i in us]
    o_refs = (of_ref, ob_ref)
    for i, (d, h) in enumerate(units):
        s_ref[0, d, h] = s_new[i]
        o_refs[d][0, 0, :, h * dv:(h + 1) * dv] = o[i]


def delta_scan(qkv, beta, g, s0):
    bsz, length, _ = qkv.shape
    _, _, heads, dk, dv = s0.shape
    assert dk == dv
    c = CHUNK
    n = length // c
    gt = jnp.swapaxes(g.reshape(bsz, 2, n, c, heads), 3, 4)

    def chunk_of(d, i):
        return i + d * (n - 1 - 2 * i)

    tok = lambda part, d: pl.BlockSpec((1, c, heads * dk), lambda b, i: (b, chunk_of(d, i), part))
    par = lambda d: pl.BlockSpec((1, 1, c, heads), lambda b, i: (b, d, chunk_of(d, i), 0))
    row = lambda d: pl.BlockSpec((1, 1, 1, heads, c), lambda b, i: (b, d, chunk_of(d, i), 0, 0))
    st = pl.BlockSpec((1, 2, heads, dk, dv), lambda b, i: (b, 0, 0, 0, 0))
    out = lambda d: pl.BlockSpec((1, 1, c, heads * dv), lambda b, i: (0, b, chunk_of(d, i), 0))
    return pl.pallas_call(
        functools.partial(_delta_body, heads=heads, dk=dk, dv=dv),
        grid=(bsz, n),
        in_specs=[tok(0, 0), tok(1, 0), tok(2, 0), tok(0, 1), tok(1, 1), tok(2, 1),
                  par(0), par(1), par(0), par(1), row(0), row(1), st],
        out_specs=[out(0), out(1), st],
        out_shape=[jax.ShapeDtypeStruct((1, bsz, length, heads * dv), F32),
                   jax.ShapeDtypeStruct((1, bsz, length, heads * dv), F32),
                   jax.ShapeDtypeStruct(s0.shape, F32)],
        compiler_params=_cparams("parallel", "arbitrary"),
    )(qkv, qkv, qkv, qkv, qkv, qkv, beta, beta, g, g, gt, gt, s0)


def _shift_rows(x, s, rows):
    length = x.shape[0]
    y = pltpu.roll(x, s % length, 0)
    return jnp.where(rows >= s if s > 0 else rows < length + s, y, 0.0)


def _short_conv(x, w_ref, rows):
    return (w_ref[0:1, :] * _shift_rows(x, 2, rows) + w_ref[1:2, :] * _shift_rows(x, 1, rows)
            + w_ref[2:3, :] * x + w_ref[3:4, :] * _shift_rows(x, -1, rows))


def _token_shift(x, mu_ref, rows):
    return (x + mu_ref[0:1, :] * (_shift_rows(x, 1, rows) - x)
            + mu_ref[1:2, :] * (_shift_rows(x, -1, rows) - x))


def _group_ones(width, group):
    i = lax.broadcasted_iota(jnp.int32, (width, width), 0) // group
    j = lax.broadcasted_iota(jnp.int32, (width, width), 1) // group
    return (i == j).astype(BF16)


def _group_sum(x, ones):
    hi = x.astype(BF16)
    lo = (x - hi.astype(F32)).astype(BF16)
    return (jnp.dot(hi, ones, preferred_element_type=F32) + jnp.dot(lo, ones, preferred_element_type=F32))


def _delta_prep_body(p_ref, w_ref, o_ref, *, dk, q_blocks):
    j = pl.program_id(1)
    rows = lax.broadcasted_iota(jnp.int32, p_ref.shape[1:], 0)
    y = jax.nn.silu(_short_conv(p_ref[0], w_ref, rows))
    norm_w = jnp.where(j < q_blocks, dk ** -0.5, jnp.where(j < 2 * q_blocks, 1.0, 0.0))
    plain_w = jnp.where(j < 2 * q_blocks, 0.0, 1.0)
    for h in range(y.shape[1] // dk):
        yh = y[:, h * dk:(h + 1) * dk]
        inv = lax.rsqrt(jnp.sum(yh * yh, axis=-1, keepdims=True) + 1e-6)
        o_ref[0, :, h * dk:(h + 1) * dk] = yh * (inv * norm_w + plain_w)


def delta_prep(p, conv_w, heads, dk, tc=512):
    bsz, length, _ = p.shape
    width = heads * dk
    spec = pl.BlockSpec((1, length, tc), lambda b, j: (b, 0, j))
    return pl.pallas_call(
        functools.partial(_delta_prep_body, dk=dk, q_blocks=width // tc),
        grid=(bsz, 3 * width // tc),
        in_specs=[spec, pl.BlockSpec((4, tc), lambda b, j: (0, j))],
        out_specs=spec,
        out_shape=jax.ShapeDtypeStruct((bsz, length, 3 * width), F32),
        compiler_params=_cparams("parallel", "parallel"),
    )(p, conv_w)


def _rwkv_prep_body(r_ref, k_ref, v_ref, sm_ref, mur_ref, muk_ref, muv_ref, mus_ref,
                    w0_ref, w2_ref, a0_ref, a2_ref, g2_ref, kkw_ref, kaw_ref,
                    ro_ref, vo_ref, kk_ref, lw_ref, kd_ref, ag_ref, gate_ref, *, hn, r_decay, r_a, r_gate):
    rows = lax.broadcasted_iota(jnp.int32, r_ref.shape[1:], 0)
    rows_s = lax.broadcasted_iota(jnp.int32, sm_ref.shape[1:], 0)
    ro_ref[0] = _token_shift(r_ref[0], mur_ref, rows)
    vo_ref[0] = _token_shift(v_ref[0], muv_ref, rows)
    kr = _token_shift(k_ref[0], muk_ref, rows)
    sm = _token_shift(sm_ref[0], mus_ref, rows_s)
    wd = jnp.tanh(sm[:, :2 * r_decay])
    ad = sm[:, 2 * r_decay:2 * r_decay + 2 * r_a]
    gd = jax.nn.sigmoid(sm[:, 2 * r_decay + 2 * r_a:2 * r_decay + 2 * r_a + r_gate])
    gate_ref[0] = _dotb(gd, g2_ref[...])
    kx = kr * kkw_ref[...]
    ones = _group_ones(kx.shape[1], hn)
    kk_ref[0] = kx * lax.rsqrt(_group_sum(kx * kx, ones) + 1e-6)
    for d in range(2):
        w_pre = w0_ref[d:d + 1, :] + _dotb(wd[:, d * r_decay:(d + 1) * r_decay], w2_ref[d])
        lw_ref[0, d] = -jnp.exp(-jax.nn.softplus(-w_pre) - 0.5)
        a_g = jax.nn.sigmoid(a0_ref[d:d + 1, :] + _dotb(ad[:, d * r_a:(d + 1) * r_a], a2_ref[d]))
        ag_ref[0, d] = a_g
        kd_ref[0, d] = kr * (1.0 + (a_g - 1.0) * kaw_ref[...])


def rwkv_prep(p, col0, small_col0, small_w, mu_rkv, mu_small, w0, w2, a0, a2, g2, k_k, k_a, hn, tc=256):
    bsz, length, _ = p.shape
    width = w0.shape[1]
    r_decay, r_a, r_gate = w2.shape[1], a2.shape[1], g2.shape[0]
    nt = width // tc
    assert col0 % tc == 0 and small_col0 % small_w == 0 and width % tc == 0

    def part(i):
        return pl.BlockSpec((1, length, tc), lambda b, j: (b, 0, col0 // tc + i * nt + j))

    def mu_part(i):
        return pl.BlockSpec((2, tc), lambda b, j: (0, i * nt + j))

    vec = pl.BlockSpec((1, tc), lambda b, j: (0, j))
    dvec = pl.BlockSpec((2, tc), lambda b, j: (0, j))
    tok = pl.BlockSpec((1, length, tc), lambda b, j: (b, 0, j))
    dtok = pl.BlockSpec((1, 2, length, tc), lambda b, j: (b, 0, 0, j))
    one = jax.ShapeDtypeStruct((bsz, length, width), F32)
    two = jax.ShapeDtypeStruct((bsz, 2, length, width), F32)
    return pl.pallas_call(
        functools.partial(_rwkv_prep_body, hn=hn, r_decay=r_decay, r_a=r_a, r_gate=r_gate),
        grid=(bsz, nt),
        in_specs=[part(0), part(1), part(2),
                  pl.BlockSpec((1, length, small_w), lambda b, j: (b, 0, small_col0 // small_w)),
                  mu_part(0), mu_part(1), mu_part(2),
                  pl.BlockSpec((2, small_w), lambda b, j: (0, 0)),
                  dvec, pl.BlockSpec((2, r_decay, tc), lambda b, j: (0, 0, j)),
                  dvec, pl.BlockSpec((2, r_a, tc), lambda b, j: (0, 0, j)),
                  pl.BlockSpec((r_gate, tc), lambda b, j: (0, j)), vec, vec],
        out_specs=[tok, tok, tok, dtok, dtok, dtok, tok],
        out_shape=[one, one, one, two, two, two, one],
        compiler_params=_cparams("parallel", "parallel"),
    )(p, p, p, p, mu_rkv, mu_rkv, mu_rkv, mu_small, w0, w2, a0, a2, g2,
      k_k.reshape(1, width), k_a.reshape(1, width))


def _even_post_body(of_ref, ob_ref, z_ref, dg_ref, yf_ref, yb_ref, r_ref, v_ref, kf_ref, kb_ref,
                    gate_ref, rk_ref, lg_ref, lb_ref, out_ref, *, dv, hn, gn_eps):
    o = of_ref[0, 0] + ob_ref[0, 0]
    wa = o.shape[1]
    z = z_ref[0]
    for h in range(wa // dv):
        sl = slice(h * dv, (h + 1) * dv)
        oh = o[:, sl]
        oh = oh * lax.rsqrt(jnp.mean(oh * oh, axis=-1, keepdims=True) + NORM_EPS) * dg_ref[...]
        out_ref[0, :, sl] = (oh * jax.nn.silu(z[:, sl])).astype(out_ref.dtype)
    y = yf_ref[0, 0] + yb_ref[0, 0]
    ones = _group_ones(LANES, hn)
    for t in range(y.shape[1] // LANES):
        sl = slice(t * LANES, (t + 1) * LANES)
        yt = y[:, sl]
        cen = yt - _group_sum(yt, ones) * (1.0 / hn)
        var = _group_sum(cen * cen, ones) * (1.0 / hn)
        yn = cen * lax.rsqrt(var + gn_eps) * lg_ref[:, sl] + lb_ref[:, sl]
        rk = r_ref[0, :, sl] * (kf_ref[0, 0, :, sl] + kb_ref[0, 0, :, sl]) * rk_ref[:, sl]
        bonus = _group_sum(rk, ones) * v_ref[0, :, sl]
        out_ref[0, :, wa + t * LANES:wa + (t + 1) * LANES] = ((yn + bonus) * gate_ref[0, :, sl]).astype(
            out_ref.dtype)


def even_post(o_f, o_b, p, z_col, dn_norm_g, y_f, y_b, r, v, kdir, gate, r_k, ln_g, ln_b, hn, tl=256):
    _, bsz, length, wa = o_f.shape
    lead = lambda w: pl.BlockSpec((1, 1, tl, w), lambda b, i: (0, b, i, 0))
    wb = y_f.shape[3]
    dv = dn_norm_g.shape[0]
    assert z_col % wa == 0
    dirs = lambda d, w: pl.BlockSpec((1, 1, tl, w), lambda b, i: (b, d, i, 0))
    tok = lambda w: pl.BlockSpec((1, tl, w), lambda b, i: (b, i, 0))
    vec = lambda w: pl.BlockSpec((1, w), lambda b, i: (0, 0))
    return pl.pallas_call(
        functools.partial(_even_post_body, dv=dv, hn=hn, gn_eps=hn * 1e-5),
        grid=(bsz, length // tl),
        in_specs=[lead(wa), lead(wa),
                  pl.BlockSpec((1, tl, wa), lambda b, i: (b, i, z_col // wa)), vec(dv),
                  lead(wb), lead(wb), tok(wb), tok(wb), dirs(0, wb), dirs(1, wb), tok(wb),
                  vec(wb), vec(wb), vec(wb)],
        out_specs=tok(wa + wb),
        out_shape=jax.ShapeDtypeStruct((bsz, length, wa + wb), BF16),
        compiler_params=_cparams("parallel", "parallel"),
    )(o_f, o_b, p, dn_norm_g.reshape(1, dv), y_f, y_b, r, v, kdir, kdir, gate,
      r_k.reshape(1, wb), ln_g.reshape(1, wb), ln_b.reshape(1, wb))


def _rwkv_body(rf_ref, vf_ref, kkf_ref, rb_ref, vb_ref, kkb_ref, lwf_ref, lwb_ref, kdf_ref, kdb_ref,
               agf_ref, agb_ref, s0_ref, yf_ref, yb_ref, s_ref, *, heads, hn):
    c = rf_ref.shape[1]

    @pl.when(pl.program_id(1) == 0)
    def _():
        s_ref[...] = s0_ref[...]

    ii, jj = _chunk_iotas(c)
    los, his = [ii, jj], [jj, ii]
    i2 = lax.broadcasted_iota(jnp.int32, (c, 2 * c), 0)
    j2 = lax.broadcasted_iota(jnp.int32, (c, 2 * c), 1) & (c - 1)
    incl2 = [i2 >= j2, j2 >= i2]
    strict = [ii > jj, jj > ii]
    a_t, r_t, b_t, k_t, b_p, k_p, p_c, v = [], [], [], [], [], [], [], []
    dir_refs = ((rf_ref, vf_ref, kkf_ref, lwf_ref, kdf_ref, agf_ref),
                (rb_ref, vb_ref, kkb_ref, lwb_ref, kdb_ref, agb_ref))
    for d, (r_ref, v_ref, kk_ref, lw_ref, kd_ref, ag_ref) in enumerate(dir_refs):
        logw = lw_ref[0, 0]
        lw = _dot_hi((los[d] >= his[d]).astype(F32), logw)
        lw_tot = jnp.sum(logw, axis=0, keepdims=True)
        kk = kk_ref[0]
        kd = kd_ref[0, 0]
        bb = kk * ag_ref[0, 0]
        e_out = jnp.exp(-lw)
        e_rem = jnp.exp(lw_tot - lw)
        a_t.append(-kk * jnp.exp(lw - logw))
        r_t.append(r_ref[0] * jnp.exp(lw))
        b_t.append(bb * e_out)
        k_t.append(kd * e_out)
        b_p.append(bb * e_rem)
        k_p.append(kd * e_rem)
        p_c.append(jnp.exp(lw_tot))
        v.append(v_ref[0])

    units = [(d, h) for d in range(2) for h in range(heads)]
    us = range(len(units))
    sl = [slice(h * hn, (h + 1) * hn) for _, h in units]
    dr = [d for d, _ in units]
    s = [s_ref[0, d, h] for d, h in units]
    lhs = [jnp.concatenate([a_t[dr[i]][:, sl[i]], r_t[dr[i]][:, sl[i]]], axis=0) for i in us]
    rhs = [jnp.concatenate([b_t[dr[i]][:, sl[i]], k_t[dr[i]][:, sl[i]]], axis=0) for i in us]
    gram = [_dotb_nt(lhs[i], rhs[i]) for i in us]
    ls = [_dotb_nt(lhs[i], s[i]) for i in us]
    t = _unit_tri_inverses([jnp.where(strict[dr[i]], -gram[i][:c, :c], 0.0) for i in us],
                           ii, jj, los, his, which=dr)
    vh = [v[dr[i]][:, sl[i]] for i in us]
    pre = [ls[i][:c] + _dotb(jnp.where(strict[dr[i]], gram[i][:c, c:], 0.0), vh[i]) for i in us]
    uv = [jnp.concatenate([_dotb(t[i], pre[i]), vh[i]], axis=0) for i in us]
    y = [ls[i][c:] + _dotb(jnp.where(incl2[dr[i]], gram[i][c:], 0.0), uv[i]) for i in us]
    s_new = [s[i] * p_c[dr[i]][:, sl[i]]
             + _dotb_tn(uv[i], jnp.concatenate([b_p[dr[i]][:, sl[i]], k_p[dr[i]][:, sl[i]]], axis=0))
             for i in us]
    y_refs = (yf_ref, yb_ref)
    for i, (d, h) in enumerate(units):
        s_ref[0, d, h] = s_new[i]
        y_refs[d][0, 0, :, sl[i]] = y[i]


def rwkv_scan(r, v, kk, logw, kdir, ag, s0):
    bsz, length, width = r.shape
    _, _, heads, hn, _ = s0.shape
    c = CHUNK
    n = length // c

    def chunk_of(d, i):
        return i + d * (n - 1 - 2 * i)

    tok = lambda d: pl.BlockSpec((1, c, width), lambda b, i: (b, chunk_of(d, i), 0))
    par = lambda d: pl.BlockSpec((1, 1, c, width), lambda b, i: (b, d, chunk_of(d, i), 0))
    out = lambda d: pl.BlockSpec((1, 1, c, width), lambda b, i: (0, b, chunk_of(d, i), 0))
    st = pl.BlockSpec((1, 2, heads, hn, hn), lambda b, i: (b, 0, 0, 0, 0))
    one_dir = jax.ShapeDtypeStruct((1, bsz, length, width), F32)
    return pl.pallas_call(
        functools.partial(_rwkv_body, heads=heads, hn=hn),
        grid=(bsz, n),
        in_specs=[tok(0), tok(0), tok(0), tok(1), tok(1), tok(1),
                  par(0), par(1), par(0), par(1), par(0), par(1), st],
        out_specs=[out(0), out(1), st],
        out_shape=[one_dir, one_dir, jax.ShapeDtypeStruct(s0.shape, F32)],
        compiler_params=_cparams("parallel", "arbitrary"),
    )(r, v, kk, r, v, kk, logw, logw, kdir, kdir, ag, ag, s0)


LRU_C = 8.0


def _scan_pass(a, b, shift, rows, rev):
    length = a.shape[0]
    if rev:
        a_s = pltpu.roll(a, length - shift, 0)
        b_s = pltpu.roll(b, length - shift, 0)
        valid = rows < length - shift
    else:
        a_s = pltpu.roll(a, shift, 0)
        b_s = pltpu.roll(b, shift, 0)
        valid = rows >= shift
    a_s = jnp.where(valid, a_s, 1.0)
    b_s = jnp.where(valid, b_s, 0.0)
    return a * a_s, a * b_s + b


SUBLANES = 8


def _blocked_scan(a, b, rows, rev, a_ref, b_ref):
    length, width = a.shape
    sub = rows & (SUBLANES - 1)
    for shift in (1, 2, 4):
        if rev:
            a_s, b_s = pltpu.roll(a, length - shift, 0), pltpu.roll(b, length - shift, 0)
            valid = sub < SUBLANES - shift
        else:
            a_s, b_s = pltpu.roll(a, shift, 0), pltpu.roll(b, shift, 0)
            valid = sub >= shift
        a, b = a * jnp.where(valid, a_s, 1.0), a * jnp.where(valid, b_s, 0.0) + b
    a_ref[...] = a
    b_ref[...] = b
    nb = length // SUBLANES
    last = 0 if rev else SUBLANES - 1
    a_blk = a_ref[pl.ds(last, nb, stride=SUBLANES), :]
    b_blk = b_ref[pl.ds(last, nb, stride=SUBLANES), :]
    brow = lax.broadcasted_iota(jnp.int32, (nb, width), 0)
    shift = 1
    while shift < nb:
        a_blk, b_blk = _scan_pass(a_blk, b_blk, shift, brow, rev)
        shift *= 2
    carry = _shift_rows(b_blk, -1 if rev else 1, brow)
    carry = jnp.broadcast_to(carry[:, None, :], (nb, SUBLANES, width)).reshape(length, width)
    return a * carry + b


def _lru_body(x_ref, gbr_ref, cw_ref, cb_ref, wg_ref, bg_ref, lam_ref, s0_ref, o_ref, s_ref, a_scr, b_scr):
    length, bs = a_scr.shape
    rows = lax.broadcasted_iota(jnp.int32, (length, bs), 0)
    for blk in range(x_ref.shape[2] // bs):
        sl = slice(blk * bs, (blk + 1) * bs)
        x = _short_conv(x_ref[0, :, sl], cw_ref[:, sl], rows) + cb_ref[:, sl]
        xb = x.astype(BF16)
        y = jnp.zeros_like(x)
        for d in range(2):
            rev = d == 1
            r_g = jax.nn.sigmoid(jnp.dot(xb, wg_ref[d, 0, blk].astype(BF16), preferred_element_type=F32)
                                 + bg_ref[d, 0:1, sl])
            i_g = jax.nn.sigmoid(jnp.dot(xb, wg_ref[d, 1, blk].astype(BF16), preferred_element_type=F32)
                                 + bg_ref[d, 1:2, sl])
            log_a = -LRU_C * r_g * jax.nn.softplus(-lam_ref[d:d + 1, sl])
            a = jnp.exp(log_a)
            b = jnp.sqrt(-jnp.tanh(log_a) * (a * a + 1.0)) * (i_g * x)
            first = (rows == (length - 1 if rev else 0))
            b = jnp.where(first, b + a * s0_ref[0, d:d + 1, sl], b)
            b = _blocked_scan(a, b, rows, rev, a_scr, b_scr)
            y = y + b
            s_ref[0, d:d + 1, sl] = b[0:1, :] if rev else b[length - 1:length, :]
        o_ref[0, :, sl] = (jax.nn.gelu(gbr_ref[0, :, sl]) * y).astype(o_ref.dtype)


def lru_mix(p, conv_w, conv_b, w_gate, b_gate, lam, s0, blocks_per_step=2):
    bsz, length, _ = p.shape
    nb, bs = w_gate.shape[2], w_gate.shape[3]
    width = nb * bs
    bps = blocks_per_step
    assert nb % bps == 0
    wide = bs * bps
    tok = pl.BlockSpec((1, length, wide), lambda b, j: (b, 0, j))
    st = pl.BlockSpec((1, 2, wide), lambda b, j: (b, 0, j))
    return pl.pallas_call(
        _lru_body,
        grid=(bsz, nb // bps),
        in_specs=[pl.BlockSpec((1, length, wide), lambda b, j: (b, 0, nb // bps + j)), tok,
                  pl.BlockSpec((4, wide), lambda b, j: (0, j)),
                  pl.BlockSpec((1, wide), lambda b, j: (0, j)),
                  pl.BlockSpec((2, 2, bps, bs, bs), lambda b, j: (0, 0, j, 0, 0)),
                  pl.BlockSpec((2, 2, wide), lambda b, j: (0, 0, j)),
                  pl.BlockSpec((2, wide), lambda b, j: (0, j)),
                  st],
        out_specs=[tok, st],
        out_shape=[jax.ShapeDtypeStruct((bsz, length, width), BF16),
                   jax.ShapeDtypeStruct((bsz, 2, width), F32)],
        scratch_shapes=[pltpu.VMEM((length, bs), F32)] * 2,
        compiler_params=_cparams("parallel", "parallel"),
    )(p, p, conv_w, conv_b.reshape(1, width), w_gate, b_gate, lam, s0)


def _dwconv(x, w9, cols, grid_w, image_rows):
    tokens = x.shape[0]
    w = lambda i, j: w9[3 * i + j:3 * i + j + 1, :]
    x_l = pltpu.roll(jnp.where(cols != grid_w - 1, x, 0.0), 1, 0)
    x_r = pltpu.roll(jnp.where(cols != 0, x, 0.0), tokens - 1, 0)
    out = w(1, 0) * x_l + w(1, 1) * x + w(1, 2) * x_r
    if image_rows > 1:
        assert tokens == grid_w * image_rows
        up = w(0, 0) * x_l + w(0, 1) * x + w(0, 2) * x_r
        dn = w(2, 0) * x_l + w(2, 1) * x + w(2, 2) * x_r
        zero = jnp.zeros((grid_w, x.shape[1]), x.dtype)
        out = out + jnp.concatenate([zero, up[:tokens - grid_w]], axis=0)
        out = out + jnp.concatenate([dn[grid_w:], zero], axis=0)
    return out


FFN_ROW_CHUNK = 256


def _ffn_up_body(h_ref, wa_ref, wb_ref, ca_ref, cb_ref, o_ref, *, grid_w, image_rows):
    cols = lax.broadcasted_iota(jnp.int32, o_ref.shape, 0) & (grid_w - 1)
    chunks = [h_ref[r:r + FFN_ROW_CHUNK, :] for r in range(0, h_ref.shape[0], FFN_ROW_CHUNK)]
    ua = jnp.concatenate([jnp.dot(hc, wa_ref[...], preferred_element_type=F32) for hc in chunks], axis=0)
    ub = jnp.concatenate([jnp.dot(hc, wb_ref[...], preferred_element_type=F32) for hc in chunks], axis=0)
    ca = _dwconv(ua, ca_ref[...], cols, grid_w, image_rows)
    cb = _dwconv(ub, cb_ref[...], cols, grid_w, image_rows)
    o_ref[...] = (jax.nn.silu(ca) * cb).astype(o_ref.dtype)


def ffn_up_conv(h, w_up, w_conv, grid_w, image_rows, tb=1024, tc=512):
    t, dm = h.shape
    f = w_up.shape[1] // 2
    nf = f // tc
    assert t % tb == 0 and f % tc == 0 and tb % (grid_w * image_rows) == 0
    assert image_rows == 1 or tb == grid_w * image_rows
    return pl.pallas_call(
        functools.partial(_ffn_up_body, grid_w=grid_w, image_rows=image_rows),
        grid=(t // tb, nf),
        in_specs=[pl.BlockSpec((tb, dm), lambda i, j: (i, 0)),
                  pl.BlockSpec((dm, tc), lambda i, j: (0, j)),
                  pl.BlockSpec((dm, tc), lambda i, j: (0, j + nf)),
                  pl.BlockSpec((9, tc), lambda i, j: (0, j)),
                  pl.BlockSpec((9, tc), lambda i, j: (0, j + nf))],
        out_specs=pl.BlockSpec((tb, tc), lambda i, j: (i, j)),
        out_shape=jax.ShapeDtypeStruct((t, f), BF16),
        compiler_params=_cparams("parallel", "arbitrary"),
    )(h, w_up, w_up, w_conv, w_conv)


def _reorder_in_proj(w_in, dn_w, heads2, rw_c, small_w):
    a_end = dn_w + 2 * heads2
    small = jnp.concatenate([w_in[:, a_end + 3 * rw_c:], w_in[:, dn_w:a_end]], axis=1)
    small = jnp.pad(small, ((0, 0), (0, small_w - small.shape[1])))
    return jnp.concatenate([w_in[:, :dn_w], w_in[:, a_end:a_end + 3 * rw_c], small], axis=1)


def _even_mixer(h, w_in, dn_conv, dn_a_log, dn_dt_bias, dn_norm_g,
                rw_mu, rw_w0, rw_w2, rw_a0, rw_a2, rw_g2, rw_k_k, rw_k_a, rw_r_k, rw_ln_g, rw_ln_b,
                s_dn, s_rw):
    bsz, length, dm = h.shape
    dn_heads, dn_dk, dn_dv = s_dn.shape[2:]
    rw_heads, rw_n = s_rw.shape[2:4]
    dn_qk, rw_c = dn_heads * dn_dk, rw_heads * rw_n
    dn_w = 3 * dn_qk + dn_heads * dn_dv
    n_small = w_in.shape[1] - dn_w - 4 * dn_heads - 3 * rw_c
    small_w = 512
    small0 = dn_w + 3 * rw_c
    w_r = _reorder_in_proj(w_in, dn_w, 2 * dn_heads, rw_c, small_w)
    p = matmul(h.reshape(bsz * length, dm), w_r).reshape(bsz, length, -1)
    b_raw = p[..., small0 + n_small:small0 + n_small + 2 * dn_heads]
    a_raw = p[..., small0 + n_small + 2 * dn_heads:small0 + n_small + 4 * dn_heads]
    beta = jax.nn.sigmoid(b_raw).reshape(bsz, length, 2, dn_heads)
    g = -jnp.exp(dn_a_log) * jax.nn.softplus(a_raw.reshape(bsz, length, 2, dn_heads) + dn_dt_bias)
    qkv = delta_prep(p, dn_conv, dn_heads, dn_dk)
    o_f, o_b, s_dn_fin = delta_scan(qkv, jnp.swapaxes(beta, 1, 2), jnp.swapaxes(g, 1, 2), s_dn)
    mu_small = jnp.pad(rw_mu[:, 3 * rw_c:], ((0, 0), (0, small_w - n_small)))
    r, vr, kk, logw, kdir, a_g, gate = rwkv_prep(
        p, dn_w, small0, small_w, rw_mu[:, :3 * rw_c], mu_small, rw_w0, rw_w2, rw_a0, rw_a2, rw_g2,
        rw_k_k, rw_k_a, rw_n)
    y_f, y_b, s_rw_fin = rwkv_scan(r, vr, kk, logw, kdir, a_g, s_rw)
    cat = even_post(o_f, o_b, p, 3 * dn_qk, dn_norm_g, y_f, y_b, r, vr, kdir, gate, rw_r_k.reshape(-1), rw_ln_g, rw_ln_b,
                    rw_n)
    return cat.reshape(bsz * length, -1), s_dn_fin, s_rw_fin


def _odd_mixer(h, w_in, conv_w, conv_b, w_gate, b_gate, lam, s_lru):
    bsz, length, dm = h.shape
    p = matmul(h.reshape(bsz * length, dm), w_in).reshape(bsz, length, -1)
    a, s_fin = lru_mix(p, conv_w, conv_b, w_gate, b_gate, lam, s_lru)
    return a.reshape(bsz * length, -1), s_fin


def _trunk(x, mod, rows, s_dn, s_rw, s_lru, prm):
    depth = mod.shape[0]
    bsz, length, dm = x.shape
    fin_dn, fin_rw, fin_lru = [], [], []
    x = x.reshape(bsz * length, dm)
    h = norm_mod(x, prm["norm_g"][0, 0], mod[0], 0)
    for l in range(depth):
        g = prm["norm_g"][l]
        h = h.reshape(bsz, length, dm)
        i = l // 2
        if l % 2 == 0:
            mix, sd, sr = _even_mixer(
                h, prm["ev_w_in"][i], prm["dn_conv"][i], prm["dn_a_log"][i],
                prm["dn_dt_bias"][i], prm["dn_norm_g"][i], prm["rw_mu"][i], prm["rw_w0"][i],
                prm["rw_w2"][i], prm["rw_a0"][i], prm["rw_a2"][i], prm["rw_g2"][i], prm["rw_k_k"][i],
                prm["rw_k_a"][i], prm["rw_r_k"][i], prm["rw_ln_g"][i], prm["rw_ln_b"][i],
                s_dn[:, i], s_rw[:, i])
            fin_dn.append(sd)
            fin_rw.append(sr)
            w_out = prm["ev_w_out"][i]
        else:
            mix, sl = _odd_mixer(h, prm["od_w_in"][i], prm["lru_conv"][i],
                                 prm["lru_conv_b"][i], prm["lru_w_gate"][i], prm["lru_b_gate"][i],
                                 prm["lru_lambda"][i], s_lru[:, i])
            fin_lru.append(sl)
            w_out = prm["od_w_out"][i]
        x, h = matmul_residual(mix, w_out.astype(BF16), x, g[1], mod[l], 2, next_norm=(g[2], mod[l], 3))
        mid = ffn_up_conv(h, prm["ffn_w_up"][l].astype(BF16), prm["ffn_conv"][l].reshape(9, -1),
                          length // rows, rows)
        nxt = (prm["norm_g"][l + 1, 0], mod[l + 1], 0) if l + 1 < depth else None
        x, h = matmul_residual(mid, prm["ffn_w_down"][l].astype(BF16), x, g[3], mod[l], 5, next_norm=nxt,
                               tm=1024, tk=512, vmem=VMEM_LIMIT_LARGE_BYTES, tail_chunk=256)
    return (x.reshape(bsz, length, dm), jnp.stack(fin_dn, axis=1), jnp.stack(fin_rw, axis=1),
            jnp.stack(fin_lru, axis=1))


def kernel(x_prompt, x_sample, state_dn, state_rwkv, state_lru, c, c_ctx, w_mod, b_mod, norm_g, ffn_w_up, ffn_conv, ffn_w_down, ev_w_in, ev_w_out, dn_conv, dn_a_log, dn_dt_bias, dn_norm_g, rw_mu, rw_w0, rw_w2, rw_a0, rw_a2, rw_g2, rw_k_k, rw_k_a, rw_r_k, rw_ln_g, rw_ln_b, od_w_in, od_w_out, lru_conv, lru_conv_b, lru_w_gate, lru_b_gate, lru_lambda):
    prm = dict(w_mod=w_mod, b_mod=b_mod, norm_g=norm_g, ffn_w_up=ffn_w_up, ffn_conv=ffn_conv,
               ffn_w_down=ffn_w_down, ev_w_in=ev_w_in, ev_w_out=ev_w_out, dn_conv=dn_conv,
               dn_a_log=dn_a_log, dn_dt_bias=dn_dt_bias, dn_norm_g=dn_norm_g, rw_mu=rw_mu, rw_w0=rw_w0,
               rw_w2=rw_w2, rw_a0=rw_a0, rw_a2=rw_a2, rw_g2=rw_g2, rw_k_k=rw_k_k, rw_k_a=rw_k_a,
               rw_r_k=rw_r_k, rw_ln_g=rw_ln_g, rw_ln_b=rw_ln_b, od_w_in=od_w_in, od_w_out=od_w_out,
               lru_conv=lru_conv, lru_conv_b=lru_conv_b, lru_w_gate=lru_w_gate, lru_b_gate=lru_b_gate,
               lru_lambda=lru_lambda)
    bp = x_prompt.shape[0]
    grid_w = 64
    depth, dm = w_mod.shape[0], w_mod.shape[1]
    cond = jnp.concatenate([c_ctx[None, :], c], axis=0)
    mod = modulation_all(jnp.pad(cond, ((0, (-cond.shape[0]) % 16), (0, 0))), w_mod, b_mod)
    mod = mod.reshape(depth, -1, 6, 1, dm)
    y_prompt, new_dn, new_rw, new_lru = _trunk(
        x_prompt, mod[:, :1], 1,
        jnp.zeros((bp,) + state_dn.shape[1:], F32),
        jnp.zeros((bp,) + state_rwkv.shape[1:], F32),
        jnp.zeros((bp,) + state_lru.shape[1:], F32), prm)
    rows = x_sample.shape[1] // grid_w
    y_sample, _, _, _ = _trunk(x_sample, mod[:, 1:1 + c.shape[0]], rows, state_dn, state_rwkv, state_lru,
                               prm)
    return (y_prompt, y_sample, new_dn, new_rw, new_lru)
```

```python
import functools
import math

import jax
import jax.numpy as jnp
from jax import lax
from jax.experimental import pallas as pl
from jax.experimental.pallas import tpu as pltpu

F32 = jnp.float32
BF16 = jnp.bfloat16
HIGHEST = lax.Precision.HIGHEST

NORM_EPS = 1e-6
CHUNK = 64
VMEM_LIMIT_BYTES = 48 * 1024 * 1024
VMEM_LIMIT_LARGE_BYTES = 56 * 1024 * 1024


def _cparams(*sem, vmem=None):
    return pltpu.CompilerParams(dimension_semantics=sem, vmem_limit_bytes=vmem or VMEM_LIMIT_BYTES)


def _dotb(a, b):
    return jnp.dot(a.astype(BF16), b.astype(BF16), preferred_element_type=F32)


def _dotb_nt(a, b):
    return lax.dot_general(a.astype(BF16), b.astype(BF16), (((1,), (1,)), ((), ())),
                           preferred_element_type=F32)


def _dotb_tn(a, b):
    return lax.dot_general(a.astype(BF16), b.astype(BF16), (((0,), (0,)), ((), ())),
                           preferred_element_type=F32)


def _dot_hi(a, b):
    return jnp.dot(a, b, preferred_element_type=F32, precision=HIGHEST)


def _mm_body(a_ref, b_ref, o_ref):
    o_ref[...] = jnp.dot(a_ref[...], b_ref[...], preferred_element_type=F32).astype(o_ref.dtype)


LANES = 128


def _tile(n, target):
    if n <= target:
        return n
    best = None
    for t in range(LANES, target + 1, LANES):
        if n % t == 0:
            best = t
    assert best is not None, n
    return best


def matmul(a, b, out_dtype=F32, tm=1024, tn=1536):
    m, k = a.shape
    n = b.shape[1]
    tm, tn = _tile(m, tm), _tile(n, tn)
    return pl.pallas_call(
        _mm_body,
        grid=(m // tm, n // tn),
        in_specs=[pl.BlockSpec((tm, k), lambda i, j: (i, 0)),
                  pl.BlockSpec((k, tn), lambda i, j: (0, j))],
        out_specs=pl.BlockSpec((tm, tn), lambda i, j: (i, j)),
        out_shape=jax.ShapeDtypeStruct((m, n), out_dtype),
        compiler_params=_cparams("parallel", "arbitrary"),
    )(a.astype(BF16), b.astype(BF16))


def _mod_body(c_ref, w_ref, b_ref, o_ref):
    o_ref[0] = jnp.dot(c_ref[...], w_ref[0].astype(BF16), preferred_element_type=F32) + b_ref[0]


def modulation_all(cvec, w_mod, b_mod, tn=1024):
    nc, dm = cvec.shape
    depth, _, n = w_mod.shape
    return pl.pallas_call(
        _mod_body,
        grid=(depth, n // tn),
        in_specs=[pl.BlockSpec((nc, dm), lambda l, j: (0, 0)),
                  pl.BlockSpec((1, dm, tn), lambda l, j: (l, 0, j)),
                  pl.BlockSpec((1, 1, tn), lambda l, j: (l, 0, j))],
        out_specs=pl.BlockSpec((1, nc, tn), lambda l, j: (l, 0, j)),
        out_shape=jax.ShapeDtypeStruct((depth, nc, n), F32),
        compiler_params=_cparams("parallel", "parallel"),
    )(jax.nn.silu(cvec).astype(BF16), w_mod, b_mod.reshape(depth, 1, n))


def _norm_mod(x, g, scale, shift):
    return x * lax.rsqrt(jnp.mean(x * x, axis=-1, keepdims=True) + NORM_EPS) * g * (1.0 + scale) + shift


def _mod_spec(which, blocks_per_cond, dm):
    return pl.BlockSpec((1, 1, 1, dm), lambda i, *_: (i // blocks_per_cond, which, 0, 0))


def _norm_mod_body(x_ref, g_ref, sc_ref, sh_ref, h_ref):
    h_ref[...] = _norm_mod(x_ref[...], g_ref[...], sc_ref[0, 0], sh_ref[0, 0]).astype(h_ref.dtype)


def norm_mod(x, g, mod, which_shift, tm=512):
    t, dm = x.shape
    bpc = t // mod.shape[0] // tm
    row = pl.BlockSpec((tm, dm), lambda i: (i, 0))
    return pl.pallas_call(
        _norm_mod_body,
        grid=(t // tm,),
        in_specs=[row, pl.BlockSpec((1, dm), lambda i: (0, 0)),
                  _mod_spec(which_shift + 1, bpc, dm), _mod_spec(which_shift, bpc, dm)],
        out_specs=row,
        out_shape=jax.ShapeDtypeStruct((t, dm), BF16),
        compiler_params=_cparams("parallel"),
    )(x, g.reshape(1, dm), mod, mod)


def _mm_res_body(a_ref, w_ref, x_ref, g_ref, gate_ref, *rest, nk, emit_h, tail_chunk):
    if emit_h:
        gn_ref, sc_ref, sh_ref, xo_ref, h_ref = rest
    else:
        (xo_ref,) = rest
    k = pl.program_id(1)

    if nk > 1:
        @pl.when(k == 0)
        def _():
            xo_ref[...] = jnp.zeros_like(xo_ref)

        @pl.when(k < nk - 1)
        def _():
            xo_ref[...] += jnp.dot(a_ref[...], w_ref[...], preferred_element_type=F32)

    @pl.when(k == nk - 1)
    def _():
        chunk = tail_chunk or a_ref.shape[0]
        for r in range(0, a_ref.shape[0], chunk):
            rows = slice(r, r + chunk)
            y = jnp.dot(a_ref[rows, :], w_ref[...], preferred_element_type=F32)
            if nk > 1:
                y = y + xo_ref[rows, :]
            y = y * lax.rsqrt(jnp.mean(y * y, axis=-1, keepdims=True) + NORM_EPS) * g_ref[...]
            xn = x_ref[rows, :] + gate_ref[0, 0] * y
            xo_ref[rows, :] = xn
            if emit_h:
                h_ref[rows, :] = _norm_mod(xn, gn_ref[...], sc_ref[0, 0], sh_ref[0, 0]).astype(h_ref.dtype)


def matmul_residual(a, w, x, g_out, mod, which_gate, next_norm=None, tm=512, tk=2048, vmem=None,
                    tail_chunk=None):
    t, kdim = a.shape
    dm = w.shape[1]
    tk = _tile(kdim, tk)
    nk = kdim // tk
    bpc = t // mod.shape[0] // tm
    row = pl.BlockSpec((tm, dm), lambda i, k: (i, 0))
    vec = pl.BlockSpec((1, dm), lambda i, k: (0, 0))
    in_specs = [pl.BlockSpec((tm, tk), lambda i, k: (i, k)),
                pl.BlockSpec((tk, dm), lambda i, k: (k, 0)),
                row, vec, _mod_spec(which_gate, bpc, dm)]
    args = [a, w, x, g_out.reshape(1, dm), mod]
    out_specs = [row]
    out_shape = [jax.ShapeDtypeStruct((t, dm), F32)]
    if next_norm is not None:
        g_next, mod_next, which_shift = next_norm
        in_specs += [vec, _mod_spec(which_shift + 1, bpc, dm), _mod_spec(which_shift, bpc, dm)]
        args += [g_next.reshape(1, dm), mod_next, mod_next]
        out_specs.append(row)
        out_shape.append(jax.ShapeDtypeStruct((t, dm), BF16))
    out = pl.pallas_call(
        functools.partial(_mm_res_body, nk=nk, emit_h=next_norm is not None, tail_chunk=tail_chunk),
        grid=(t // tm, nk),
        in_specs=in_specs,
        out_specs=out_specs,
        out_shape=out_shape,
        compiler_params=_cparams("parallel", "arbitrary", vmem=vmem),
    )(*args)
    return out if next_norm is not None else (out[0], None)


def _chunk_iotas(c):
    return lax.broadcasted_iota(jnp.int32, (c, c), 0), lax.broadcasted_iota(jnp.int32, (c, c), 1)


def _unit_tri_inverses(ms, ii, jj, lo, hi, which=None):
    c = ms[0].shape[0]
    eye = (ii == jj).astype(F32)
    if which is None:
        lo, hi, which = [lo], [hi], [0] * len(ms)

    def pair_masks(shift):
        out = []
        for l, h in zip(lo, hi):
            same = (l >> (shift + 1)) == (h >> (shift + 1))
            out.append(same & (((l >> shift) & 1) == 1) & (((h >> shift) & 1) == 0))
        return out

    masks = pair_masks(0)
    ts = [eye - jnp.where(masks[w], m, 0.0) for m, w in zip(ms, which)]
    shift = 1
    while (1 << shift) < c:
        masks = pair_masks(shift)
        tb = [_dotb(t, jnp.where(masks[w], m, 0.0)) for t, m, w in zip(ts, ms, which)]
        ts = [t - _dotb(x, t) for x, t in zip(tb, ts)]
        shift += 1
    return ts


def _delta_body(qf_ref, kf_ref, vf_ref, qb_ref, kb_ref, vb_ref, betaf_ref, betab_ref, gf_ref, gb_ref,
                gtf_ref, gtb_ref, s0_ref, of_ref, ob_ref, s_ref, *, heads, dk, dv):
    c = qf_ref.shape[1]

    @pl.when(pl.program_id(1) == 0)
    def _():
        s_ref[...] = s0_ref[...]

    ii, jj = _chunk_iotas(c)
    los, his = [ii, jj], [jj, ii]
    units = [(d, h) for d in range(2) for h in range(heads)]
    q_refs, k_refs, v_refs = (qf_ref, qb_ref), (kf_ref, kb_ref), (vf_ref, vb_ref)
    gc_col, gc_row, g_tot, beta, incl, strict = [], [], [], [], [], []
    for d, (g_ref, gt_ref, b_ref) in enumerate(((gf_ref, gtf_ref, betaf_ref), (gb_ref, gtb_ref, betab_ref))):
        incl.append(los[d] >= his[d])
        strict.append(los[d] > his[d])
        m_incl = incl[d].astype(F32)
        g = g_ref[0, 0]
        gc_col.append(_dot_hi(m_incl, g))
        gc_row.append(lax.dot_general(gt_ref[0, 0, 0], m_incl, (((1,), (1,)), ((), ())),
                                      preferred_element_type=F32, precision=HIGHEST))
        g_tot.append(jnp.sum(g, axis=0, keepdims=True))
        beta.append(b_ref[0, 0])

    q = [q_refs[d][0, :, h * dk:(h + 1) * dk] for d, h in units]
    k = [k_refs[d][0, :, h * dk:(h + 1) * dk] for d, h in units]
    v = [v_refs[d][0, :, h * dv:(h + 1) * dv] for d, h in units]
    s = [s_ref[0, d, h] for d, h in units]
    gcc = [gc_col[d][:, h:h + 1] for d, h in units]
    bc = [beta[d][:, h:h + 1] for d, h in units]
    gt = [g_tot[d][:, h:h + 1] for d, h in units]
    us = range(len(units))
    dec_incl = []
    for i, (d, h) in enumerate(units):
        diff = gcc[i] - gc_row[d][h:h + 1, :]
        dec_incl.append(jnp.where(incl[d], jnp.exp(jnp.where(incl[d], diff, 0.0)), 0.0))
    kk = [_dotb_nt(k[i], k[i]) for i in us]
    qk = [_dotb_nt(q[i], k[i]) * dec_incl[i] for i in us]
    m = [bc[i] * kk[i] * jnp.where(strict[units[i][0]], dec_incl[i], 0.0) for i in us]
    t = _unit_tri_inverses(m, ii, jj, los, his, which=[d for d, _ in units])
    egc = [jnp.exp(gcc[i]) for i in us]
    sol = [_dotb(t[i], jnp.concatenate([bc[i] * v[i], (bc[i] * egc[i]) * k[i]], axis=1)) for i in us]
    qs = [_dotb(q[i] * egc[i], s[i]) for i in us]
    u = [sol[i][:, :dv] - _dotb(sol[i][:, dv:], s[i]) for i in us]
    o = [qs[i] + _dotb(qk[i], u[i]) for i in us]
    s_new = [jnp.exp(gt[i]) * s[i] + _dotb_tn(k[i] * jnp.exp(gt[i] - gcc[i]), u[i]) for i in us]
    o_refs = (of_ref, ob_ref)
    for i, (d, h) in enumerate(units):
        s_ref[0, d, h] = s_new[i]
        o_refs[d][0, 0, :, h * dv:(h + 1) * dv] = o[i]


def delta_scan(qkv, beta, g, s0):
    bsz, length, _ = qkv.shape
    _, _, heads, dk, dv = s0.shape
    assert dk == dv
    c = CHUNK
    n = length // c
    gt = jnp.swapaxes(g.reshape(bsz, 2, n, c, heads), 3, 4)

    def chunk_of(d, i):
        return i + d * (n - 1 - 2 * i)

    tok = lambda part, d: pl.BlockSpec((1, c, heads * dk), lambda b, i: (b, chunk_of(d, i), part))
    par = lambda d: pl.BlockSpec((1, 1, c, heads), lambda b, i: (b, d, chunk_of(d, i), 0))
    row = lambda d: pl.BlockSpec((1, 1, 1, heads, c), lambda b, i: (b, d, chunk_of(d, i), 0, 0))
    st = pl.BlockSpec((1, 2, heads, dk, dv), lambda b, i: (b, 0, 0, 0, 0))
    out = lambda d: pl.BlockSpec((1, 1, c, heads * dv), lambda b, i: (0, b, chunk_of(d, i), 0))
    return pl.pallas_call(
        functools.partial(_delta_body, heads=heads, dk=dk, dv=dv),
        grid=(bsz, n),
        in_specs=[tok(0, 0), tok(1, 0), tok(2, 0), tok(0, 1), tok(1, 1), tok(2, 1),
                  par(0), par(1), par(0), par(1), row(0), row(1), st],
        out_specs=[out(0), out(1), st],
        out_shape=[jax.ShapeDtypeStruct((1, bsz, length, heads * dv), F32),
                   jax.ShapeDtypeStruct((1, bsz, length, heads * dv), F32),
                   jax.ShapeDtypeStruct(s0.shape, F32)],
        compiler_params=_cparams("parallel", "arbitrary"),
    )(qkv, qkv, qkv, qkv, qkv, qkv, beta, beta, g, g, gt, gt, s0)


def _shift_rows(x, s, rows):
    length = x.shape[0]
    y = pltpu.roll(x, s % length, 0)
    return jnp.where(rows >= s if s > 0 else rows < length + s, y, 0.0)


def _short_conv(x, w_ref, rows):
    return (w_ref[0:1, :] * _shift_rows(x, 2, rows) + w_ref[1:2, :] * _shift_rows(x, 1, rows)
            + w_ref[2:3, :] * x + w_ref[3:4, :] * _shift_rows(x, -1, rows))


def _token_shift(x, mu_ref, rows):
    return (x + mu_ref[0:1, :] * (_shift_rows(x, 1, rows) - x)
            + mu_ref[1:2, :] * (_shift_rows(x, -1, rows) - x))


def _group_ones(width, group):
    i = lax.broadcasted_iota(jnp.int32, (width, width), 0) // group
    j = lax.broadcasted_iota(jnp.int32, (width, width), 1) // group
    return (i == j).astype(BF16)


def _group_sum(x, ones):
    hi = x.astype(BF16)
    lo = (x - hi.astype(F32)).astype(BF16)
    return (jnp.dot(hi, ones, preferred_element_type=F32) + jnp.dot(lo, ones, preferred_element_type=F32))


def _delta_prep_body(p_ref, w_ref, o_ref, *, dk, q_blocks):
    j = pl.program_id(1)
    rows = lax.broadcasted_iota(jnp.int32, p_ref.shape[1:], 0)
    y = jax.nn.silu(_short_conv(p_ref[0], w_ref, rows))
    norm_w = jnp.where(j < q_blocks, dk ** -0.5, jnp.where(j < 2 * q_blocks, 1.0, 0.0))
    plain_w = jnp.where(j < 2 * q_blocks, 0.0, 1.0)
    for h in range(y.shape[1] // dk):
        yh = y[:, h * dk:(h + 1) * dk]
        inv = lax.rsqrt(jnp.sum(yh * yh, axis=-1, keepdims=True) + 1e-6)
        o_ref[0, :, h * dk:(h + 1) * dk] = yh * (inv * norm_w + plain_w)


def delta_prep(p, conv_w, heads, dk, tc=512):
    bsz, length, _ = p.shape
    width = heads * dk
    spec = pl.BlockSpec((1, length, tc), lambda b, j: (b, 0, j))
    return pl.pallas_call(
        functools.partial(_delta_prep_body, dk=dk, q_blocks=width // tc),
        grid=(bsz, 3 * width // tc),
        in_specs=[spec, pl.BlockSpec((4, tc), lambda b, j: (0, j))],
        out_specs=spec,
        out_shape=jax.ShapeDtypeStruct((bsz, length, 3 * width), F32),
        compiler_params=_cparams("parallel", "parallel"),
    )(p, conv_w)


def _rwkv_prep_body(r_ref, k_ref, v_ref, sm_ref, mur_ref, muk_ref, muv_ref, mus_ref,
                    w0_ref, w2_ref, a0_ref, a2_ref, g2_ref, kkw_ref, kaw_ref,
                    ro_ref, vo_ref, kk_ref, lw_ref, kd_ref, ag_ref, gate_ref, *, hn, r_decay, r_a, r_gate):
    rows = lax.broadcasted_iota(jnp.int32, r_ref.shape[1:], 0)
    rows_s = lax.broadcasted_iota(jnp.int32, sm_ref.shape[1:], 0)
    ro_ref[0] = _token_shift(r_ref[0], mur_ref, rows)
    vo_ref[0] = _token_shift(v_ref[0], muv_ref, rows)
    kr = _token_shift(k_ref[0], muk_ref, rows)
    sm = _token_shift(sm_ref[0], mus_ref, rows_s)
    wd = jnp.tanh(sm[:, :2 * r_decay])
    ad = sm[:, 2 * r_decay:2 * r_decay + 2 * r_a]
    gd = jax.nn.sigmoid(sm[:, 2 * r_decay + 2 * r_a:2 * r_decay + 2 * r_a + r_gate])
    gate_ref[0] = _dotb(gd, g2_ref[...])
    kx = kr * kkw_ref[...]
    ones = _group_ones(kx.shape[1], hn)
    kk_ref[0] = kx * lax.rsqrt(_group_sum(kx * kx, ones) + 1e-6)
    for d in range(2):
        w_pre = w0_ref[d:d + 1, :] + _dotb(wd[:, d * r_decay:(d + 1) * r_decay], w2_ref[d])
        lw_ref[0, d] = -jnp.exp(-jax.nn.softplus(-w_pre) - 0.5)
        a_g = jax.nn.sigmoid(a0_ref[d:d + 1, :] + _dotb(ad[:, d * r_a:(d + 1) * r_a], a2_ref[d]))
        ag_ref[0, d] = a_g
        kd_ref[0, d] = kr * (1.0 + (a_g - 1.0) * kaw_ref[...])


def rwkv_prep(p, col0, small_col0, small_w, mu_rkv, mu_small, w0, w2, a0, a2, g2, k_k, k_a, hn, tc=256):
    bsz, length, _ = p.shape
    width = w0.shape[1]
    r_decay, r_a, r_gate = w2.shape[1], a2.shape[1], g2.shape[0]
    nt = width // tc
    assert col0 % tc == 0 and small_col0 % small_w == 0 and width % tc == 0

    def part(i):
        return pl.BlockSpec((1, length, tc), lambda b, j: (b, 0, col0 // tc + i * nt + j))

    def mu_part(i):
        return pl.BlockSpec((2, tc), lambda b, j: (0, i * nt + j))

    vec = pl.BlockSpec((1, tc), lambda b, j: (0, j))
    dvec = pl.BlockSpec((2, tc), lambda b, j: (0, j))
    tok = pl.BlockSpec((1, length, tc), lambda b, j: (b, 0, j))
    dtok = pl.BlockSpec((1, 2, length, tc), lambda b, j: (b, 0, 0, j))
    one = jax.ShapeDtypeStruct((bsz, length, width), F32)
    two = jax.ShapeDtypeStruct((bsz, 2, length, width), F32)
    return pl.pallas_call(
        functools.partial(_rwkv_prep_body, hn=hn, r_decay=r_decay, r_a=r_a, r_gate=r_gate),
        grid=(bsz, nt),
        in_specs=[part(0), part(1), part(2),
                  pl.BlockSpec((1, length, small_w), lambda b, j: (b, 0, small_col0 // small_w)),
                  mu_part(0), mu_part(1), mu_part(2),
                  pl.BlockSpec((2, small_w), lambda b, j: (0, 0)),
                  dvec, pl.BlockSpec((2, r_decay, tc), lambda b, j: (0, 0, j)),
                  dvec, pl.BlockSpec((2, r_a, tc), lambda b, j: (0, 0, j)),
                  pl.BlockSpec((r_gate, tc), lambda b, j: (0, j)), vec, vec],
        out_specs=[tok, tok, tok, dtok, dtok, dtok, tok],
        out_shape=[one, one, one, two, two, two, one],
        compiler_params=_cparams("parallel", "parallel"),
    )(p, p, p, p, mu_rkv, mu_rkv, mu_rkv, mu_small, w0, w2, a0, a2, g2,
      k_k.reshape(1, width), k_a.reshape(1, width))


def _even_post_body(of_ref, ob_ref, z_ref, dg_ref, yf_ref, yb_ref, r_ref, v_ref, kf_ref, kb_ref,
                    gate_ref, rk_ref, lg_ref, lb_ref, out_ref, *, dv, hn, gn_eps):
    o = of_ref[0, 0] + ob_ref[0, 0]
    wa = o.shape[1]
    z = z_ref[0]
    for h in range(wa // dv):
        sl = slice(h * dv, (h + 1) * dv)
        oh = o[:, sl]
        oh = oh * lax.rsqrt(jnp.mean(oh * oh, axis=-1, keepdims=True) + NORM_EPS) * dg_ref[...]
        out_ref[0, :, sl] = (oh * jax.nn.silu(z[:, sl])).astype(out_ref.dtype)
    y = yf_ref[0, 0] + yb_ref[0, 0]
    ones = _group_ones(LANES, hn)
    for t in range(y.shape[1] // LANES):
        sl = slice(t * LANES, (t + 1) * LANES)
        yt = y[:, sl]
        cen = yt - _group_sum(yt, ones) * (1.0 / hn)
        var = _group_sum(cen * cen, ones) * (1.0 / hn)
        yn = cen * lax.rsqrt(var + gn_eps) * lg_ref[:, sl] + lb_ref[:, sl]
        rk = r_ref[0, :, sl] * (kf_ref[0, 0, :, sl] + kb_ref[0, 0, :, sl]) * rk_ref[:, sl]
        bonus = _group_sum(rk, ones) * v_ref[0, :, sl]
        out_ref[0, :, wa + t * LANES:wa + (t + 1) * LANES] = ((yn + bonus) * gate_ref[0, :, sl]).astype(
            out_ref.dtype)


def even_post(o_f, o_b, p, z_col, dn_norm_g, y_f, y_b, r, v, kdir, gate, r_k, ln_g, ln_b, hn, tl=256):
    _, bsz, length, wa = o_f.shape
    lead = lambda w: pl.BlockSpec((1, 1, tl, w), lambda b, i: (0, b, i, 0))
    wb = y_f.shape[3]
    dv = dn_norm_g.shape[0]
    assert z_col % wa == 0
    dirs = lambda d, w: pl.BlockSpec((1, 1, tl, w), lambda b, i: (b, d, i, 0))
    tok = lambda w: pl.BlockSpec((1, tl, w), lambda b, i: (b, i, 0))
    vec = lambda w: pl.BlockSpec((1, w), lambda b, i: (0, 0))
    return pl.pallas_call(
        functools.partial(_even_post_body, dv=dv, hn=hn, gn_eps=hn * 1e-5),
        grid=(bsz, length // tl),
        in_specs=[lead(wa), lead(wa),
                  pl.BlockSpec((1, tl, wa), lambda b, i: (b, i, z_col // wa)), vec(dv),
                  lead(wb), lead(wb), tok(wb), tok(wb), dirs(0, wb), dirs(1, wb), tok(wb),
                  vec(wb), vec(wb), vec(wb)],
        out_specs=tok(wa + wb),
        out_shape=jax.ShapeDtypeStruct((bsz, length, wa + wb), BF16),
        compiler_params=_cparams("parallel", "parallel"),
    )(o_f, o_b, p, dn_norm_g.reshape(1, dv), y_f, y_b, r, v, kdir, kdir, gate,
      r_k.reshape(1, wb), ln_g.reshape(1, wb), ln_b.reshape(1, wb))


def _rwkv_body(rf_ref, vf_ref, kkf_ref, rb_ref, vb_ref, kkb_ref, lwf_ref, lwb_ref, kdf_ref, kdb_ref,
               agf_ref, agb_ref, s0_ref, yf_ref, yb_ref, s_ref, *, heads, hn):
    c = rf_ref.shape[1]

    @pl.when(pl.program_id(1) == 0)
    def _():
        s_ref[...] = s0_ref[...]

    ii, jj = _chunk_iotas(c)
    los, his = [ii, jj], [jj, ii]
    i2 = lax.broadcasted_iota(jnp.int32, (c, 2 * c), 0)
    j2 = lax.broadcasted_iota(jnp.int32, (c, 2 * c), 1) & (c - 1)
    incl2 = [i2 >= j2, j2 >= i2]
    strict = [ii > jj, jj > ii]
    a_t, r_t, b_t, k_t, b_p, k_p, p_c, v = [], [], [], [], [], [], [], []
    dir_refs = ((rf_ref, vf_ref, kkf_ref, lwf_ref, kdf_ref, agf_ref),
                (rb_ref, vb_ref, kkb_ref, lwb_ref, kdb_ref, agb_ref))
    for d, (r_ref, v_ref, kk_ref, lw_ref, kd_ref, ag_ref) in enumerate(dir_refs):
        logw = lw_ref[0, 0]
        lw = _dot_hi((los[d] >= his[d]).astype(F32), logw)
        lw_tot = jnp.sum(logw, axis=0, keepdims=True)
        kk = kk_ref[0]
        kd = kd_ref[0, 0]
        bb = kk * ag_ref[0, 0]
        e_out = jnp.exp(-lw)
        e_rem = jnp.exp(lw_tot - lw)
        a_t.append(-kk * jnp.exp(lw - logw))
        r_t.append(r_ref[0] * jnp.exp(lw))
        b_t.append(bb * e_out)
        k_t.append(kd * e_out)
        b_p.append(bb * e_rem)
        k_p.append(kd * e_rem)
        p_c.append(jnp.exp(lw_tot))
        v.append(v_ref[0])

    units = [(d, h) for d in range(2) for h in range(heads)]
    us = range(len(units))
    sl = [slice(h * hn, (h + 1) * hn) for _, h in units]
    dr = [d for d, _ in units]
    s = [s_ref[0, d, h] for d, h in units]
    lhs = [jnp.concatenate([a_t[dr[i]][:, sl[i]], r_t[dr[i]][:, sl[i]]], axis=0) for i in us]
    rhs = [jnp.concatenate([b_t[dr[i]][:, sl[i]], k_t[dr[i]][:, sl[i]]], axis=0) for i in us]
    gram = [_dotb_nt(lhs[i], rhs[i]) for i in us]
    ls = [_dotb_nt(lhs[i], s[i]) for i in us]
    t = _unit_tri_inverses([jnp.where(strict[dr[i]], -gram[i][:c, :c], 0.0) for i in us],
                           ii, jj, los, his, which=dr)
    vh = [v[dr[i]][:, sl[i]] for i in us]
    pre = [ls[i][:c] + _dotb(jnp.where(strict[dr[i]], gram[i][:c, c:], 0.0), vh[i]) for i in us]
    uv = [jnp.concatenate([_dotb(t[i], pre[i]), vh[i]], axis=0) for i in us]
    y = [ls[i][c:] + _dotb(jnp.where(incl2[dr[i]], gram[i][c:], 0.0), uv[i]) for i in us]
    s_new = [s[i] * p_c[dr[i]][:, sl[i]]
             + _dotb_tn(uv[i], jnp.concatenate([b_p[dr[i]][:, sl[i]], k_p[dr[i]][:, sl[i]]], axis=0))
             for i in us]
    y_refs = (yf_ref, yb_ref)
    for i, (d, h) in enumerate(units):
        s_ref[0, d, h] = s_new[i]
        y_refs[d][0, 0, :, sl[i]] = y[i]


def rwkv_scan(r, v, kk, logw, kdir, ag, s0):
    bsz, length, width = r.shape
    _, _, heads, hn, _ = s0.shape
    c = CHUNK
    n = length // c

    def chunk_of(d, i):
        return i + d * (n - 1 - 2 * i)

    tok = lambda d: pl.BlockSpec((1, c, width), lambda b, i: (b, chunk_of(d, i), 0))
    par = lambda d: pl.BlockSpec((1, 1, c, width), lambda b, i: (b, d, chunk_of(d, i), 0))
    out = lambda d: pl.BlockSpec((1, 1, c, width), lambda b, i: (0, b, chunk_of(d, i), 0))
    st = pl.BlockSpec((1, 2, heads, hn, hn), lambda b, i: (b, 0, 0, 0, 0))
    one_dir = jax.ShapeDtypeStruct((1, bsz, length, width), F32)
    return pl.pallas_call(
        functools.partial(_rwkv_body, heads=heads, hn=hn),
        grid=(bsz, n),
        in_specs=[tok(0), tok(0), tok(0), tok(1), tok(1), tok(1),
                  par(0), par(1), par(0), par(1), par(0), par(1), st],
        out_specs=[out(0), out(1), st],
        out_shape=[one_dir, one_dir, jax.ShapeDtypeStruct(s0.shape, F32)],
        compiler_params=_cparams("parallel", "arbitrary"),
    )(r, v, kk, r, v, kk, logw, logw, kdir, kdir, ag, ag, s0)


LRU_C = 8.0


def _scan_pass(a, b, shift, rows, rev):
    length = a.shape[0]
    if rev:
        a_s = pltpu.roll(a, length - shift, 0)
        b_s = pltpu.roll(b, length - shift, 0)
        valid = rows < length - shift
    else:
        a_s = pltpu.roll(a, shift, 0)
        b_s = pltpu.roll(b, shift, 0)
        valid = rows >= shift
    a_s = jnp.where(valid, a_s, 1.0)
    b_s = jnp.where(valid, b_s, 0.0)
    return a * a_s, a * b_s + b


SUBLANES = 8


def _blocked_scan(a, b, rows, rev, a_ref, b_ref):
    length, width = a.shape
    sub = rows & (SUBLANES - 1)
    for shift in (1, 2, 4):
        if rev:
            a_s, b_s = pltpu.roll(a, length - shift, 0), pltpu.roll(b, length - shift, 0)
            valid = sub < SUBLANES - shift
        else:
            a_s, b_s = pltpu.roll(a, shift, 0), pltpu.roll(b, shift, 0)
            valid = sub >= shift
        a, b = a * jnp.where(valid, a_s, 1.0), a * jnp.where(valid, b_s, 0.0) + b
    a_ref[...] = a
    b_ref[...] = b
    nb = length // SUBLANES
    last = 0 if rev else SUBLANES - 1
    a_blk = a_ref[pl.ds(last, nb, stride=SUBLANES), :]
    b_blk = b_ref[pl.ds(last, nb, stride=SUBLANES), :]
    brow = lax.broadcasted_iota(jnp.int32, (nb, width), 0)
    shift = 1
    while shift < nb:
        a_blk, b_blk = _scan_pass(a_blk, b_blk, shift, brow, rev)
        shift *= 2
    carry = _shift_rows(b_blk, -1 if rev else 1, brow)
    carry = jnp.broadcast_to(carry[:, None, :], (nb, SUBLANES, width)).reshape(length, width)
    return a * carry + b


def _lru_body(x_ref, gbr_ref, cw_ref, cb_ref, wg_ref, bg_ref, lam_ref, s0_ref, o_ref, s_ref, a_scr, b_scr):
    length, bs = a_scr.shape
    rows = lax.broadcasted_iota(jnp.int32, (length, bs), 0)
    for blk in range(x_ref.shape[2] // bs):
        sl = slice(blk * bs, (blk + 1) * bs)
        x = _short_conv(x_ref[0, :, sl], cw_ref[:, sl], rows) + cb_ref[:, sl]
        xb = x.astype(BF16)
        y = jnp.zeros_like(x)
        for d in range(2):
            rev = d == 1
            r_g = jax.nn.sigmoid(jnp.dot(xb, wg_ref[d, 0, blk].astype(BF16), preferred_element_type=F32)
                                 + bg_ref[d, 0:1, sl])
            i_g = jax.nn.sigmoid(jnp.dot(xb, wg_ref[d, 1, blk].astype(BF16), preferred_element_type=F32)
                                 + bg_ref[d, 1:2, sl])
            log_a = -LRU_C * r_g * jax.nn.softplus(-lam_ref[d:d + 1, sl])
            a = jnp.exp(log_a)
            b = jnp.sqrt(-jnp.tanh(log_a) * (a * a + 1.0)) * (i_g * x)
            first = (rows == (length - 1 if rev else 0))
            b = jnp.where(first, b + a * s0_ref[0, d:d + 1, sl], b)
            b = _blocked_scan(a, b, rows, rev, a_scr, b_scr)
            y = y + b
            s_ref[0, d:d + 1, sl] = b[0:1, :] if rev else b[length - 1:length, :]
        o_ref[0, :, sl] = (jax.nn.gelu(gbr_ref[0, :, sl]) * y).astype(o_ref.dtype)


def lru_mix(p, conv_w, conv_b, w_gate, b_gate, lam, s0, blocks_per_step=2):
    bsz, length, _ = p.shape
    nb, bs = w_gate.shape[2], w_gate.shape[3]
    width = nb * bs
    bps = blocks_per_step
    assert nb % bps == 0
    wide = bs * bps
    tok = pl.BlockSpec((1, length, wide), lambda b, j: (b, 0, j))
    st = pl.BlockSpec((1, 2, wide), lambda b, j: (b, 0, j))
    return pl.pallas_call(
        _lru_body,
        grid=(bsz, nb // bps),
        in_specs=[pl.BlockSpec((1, length, wide), lambda b, j: (b, 0, nb // bps + j)), tok,
                  pl.BlockSpec((4, wide), lambda b, j: (0, j)),
                  pl.BlockSpec((1, wide), lambda b, j: (0, j)),
                  pl.BlockSpec((2, 2, bps, bs, bs), lambda b, j: (0, 0, j, 0, 0)),
                  pl.BlockSpec((2, 2, wide), lambda b, j: (0, 0, j)),
                  pl.BlockSpec((2, wide), lambda b, j: (0, j)),
                  st],
        out_specs=[tok, st],
        out_shape=[jax.ShapeDtypeStruct((bsz, length, width), BF16),
                   jax.ShapeDtypeStruct((bsz, 2, width), F32)],
        scratch_shapes=[pltpu.VMEM((length, bs), F32)] * 2,
        compiler_params=_cparams("parallel", "parallel"),
    )(p, p, conv_w, conv_b.reshape(1, width), w_gate, b_gate, lam, s0)


def _dwconv(x, w9, cols, grid_w, image_rows):
    tokens = x.shape[0]
    w = lambda i, j: w9[3 * i + j:3 * i + j + 1, :]
    x_l = pltpu.roll(jnp.where(cols != grid_w - 1, x, 0.0), 1, 0)
    x_r = pltpu.roll(jnp.where(cols != 0, x, 0.0), tokens - 1, 0)
    out = w(1, 0) * x_l + w(1, 1) * x + w(1, 2) * x_r
    if image_rows > 1:
        assert tokens == grid_w * image_rows
        up = w(0, 0) * x_l + w(0, 1) * x + w(0, 2) * x_r
        dn = w(2, 0) * x_l + w(2, 1) * x + w(2, 2) * x_r
        zero = jnp.zeros((grid_w, x.shape[1]), x.dtype)
        out = out + jnp.concatenate([zero, up[:tokens - grid_w]], axis=0)
        out = out + jnp.concatenate([dn[grid_w:], zero], axis=0)
    return out


FFN_ROW_CHUNK = 256


def _ffn_up_body(h_ref, wa_ref, wb_ref, ca_ref, cb_ref, o_ref, *, grid_w, image_rows):
    cols = lax.broadcasted_iota(jnp.int32, o_ref.shape, 0) & (grid_w - 1)
    chunks = [h_ref[r:r + FFN_ROW_CHUNK, :] for r in range(0, h_ref.shape[0], FFN_ROW_CHUNK)]
    ua = jnp.concatenate([jnp.dot(hc, wa_ref[...], preferred_element_type=F32) for hc in chunks], axis=0)
    ub = jnp.concatenate([jnp.dot(hc, wb_ref[...], preferred_element_type=F32) for hc in chunks], axis=0)
    ca = _dwconv(ua, ca_ref[...], cols, grid_w, image_rows)
    cb = _dwconv(ub, cb_ref[...], cols, grid_w, image_rows)
    o_ref[...] = (jax.nn.silu(ca) * cb).astype(o_ref.dtype)


def ffn_up_conv(h, w_up, w_conv, grid_w, image_rows, tb=1024, tc=512):
    t, dm = h.shape
    f = w_up.shape[1] // 2
    nf = f // tc
    assert t % tb == 0 and f % tc == 0 and tb % (grid_w * image_rows) == 0
    assert image_rows == 1 or tb == grid_w * image_rows
    return pl.pallas_call(
        functools.partial(_ffn_up_body, grid_w=grid_w, image_rows=image_rows),
        grid=(t // tb, nf),
        in_specs=[pl.BlockSpec((tb, dm), lambda i, j: (i, 0)),
                  pl.BlockSpec((dm, tc), lambda i, j: (0, j)),
                  pl.BlockSpec((dm, tc), lambda i, j: (0, j + nf)),
                  pl.BlockSpec((9, tc), lambda i, j: (0, j)),
                  pl.BlockSpec((9, tc), lambda i, j: (0, j + nf))],
        out_specs=pl.BlockSpec((tb, tc), lambda i, j: (i, j)),
        out_shape=jax.ShapeDtypeStruct((t, f), BF16),
        compiler_params=_cparams("parallel", "arbitrary"),
    )(h, w_up, w_up, w_conv, w_conv)


def _reorder_in_proj(w_in, dn_w, heads2, rw_c, small_w):
    a_end = dn_w + 2 * heads2
    small = jnp.concatenate([w_in[:, a_end + 3 * rw_c:], w_in[:, dn_w:a_end]], axis=1)
    small = jnp.pad(small, ((0, 0), (0, small_w - small.shape[1])))
    return jnp.concatenate([w_in[:, :dn_w], w_in[:, a_end:a_end + 3 * rw_c], small], axis=1)


def _even_mixer(h, w_in, dn_conv, dn_a_log, dn_dt_bias, dn_norm_g,
                rw_mu, rw_w0, rw_w2, rw_a0, rw_a2, rw_g2, rw_k_k, rw_k_a, rw_r_k, rw_ln_g, rw_ln_b,
                s_dn, s_rw):
    bsz, length, dm = h.shape
    dn_heads, dn_dk, dn_dv = s_dn.shape[2:]
    rw_heads, rw_n = s_rw.shape[2:4]
    dn_qk, rw_c = dn_heads * dn_dk, rw_heads * rw_n
    dn_w = 3 * dn_qk + dn_heads * dn_dv
    n_small = w_in.shape[1] - dn_w - 4 * dn_heads - 3 * rw_c
    small_w = 512
    small0 = dn_w + 3 * rw_c
    w_r = _reorder_in_proj(w_in, dn_w, 2 * dn_heads, rw_c, small_w)
    p = matmul(h.reshape(bsz * length, dm), w_r).reshape(bsz, length, -1)
    b_raw = p[..., small0 + n_small:small0 + n_small + 2 * dn_heads]
    a_raw = p[..., small0 + n_small + 2 * dn_heads:small0 + n_small + 4 * dn_heads]
    beta = jax.nn.sigmoid(b_raw).reshape(bsz, length, 2, dn_heads)
    g = -jnp.exp(dn_a_log) * jax.nn.softplus(a_raw.reshape(bsz, length, 2, dn_heads) + dn_dt_bias)
    qkv = delta_prep(p, dn_conv, dn_heads, dn_dk)
    o_f, o_b, s_dn_fin = delta_scan(qkv, jnp.swapaxes(beta, 1, 2), jnp.swapaxes(g, 1, 2), s_dn)
    mu_small = jnp.pad(rw_mu[:, 3 * rw_c:], ((0, 0), (0, small_w - n_small)))
    r, vr, kk, logw, kdir, a_g, gate = rwkv_prep(
        p, dn_w, small0, small_w, rw_mu[:, :3 * rw_c], mu_small, rw_w0, rw_w2, rw_a0, rw_a2, rw_g2,
        rw_k_k, rw_k_a, rw_n)
    y_f, y_b, s_rw_fin = rwkv_scan(r, vr, kk, logw, kdir, a_g, s_rw)
    cat = even_post(o_f, o_b, p, 3 * dn_qk, dn_norm_g, y_f, y_b, r, vr, kdir, gate, rw_r_k.reshape(-1), rw_ln_g, rw_ln_b,
                    rw_n)
    return cat.reshape(bsz * length, -1), s_dn_fin, s_rw_fin


def _odd_mixer(h, w_in, conv_w, conv_b, w_gate, b_gate, lam, s_lru):
    bsz, length, dm = h.shape
    p = matmul(h.reshape(bsz * length, dm), w_in).reshape(bsz, length, -1)
    a, s_fin = lru_mix(p, conv_w, conv_b, w_gate, b_gate, lam, s_lru)
    return a.reshape(bsz * length, -1), s_fin


def _trunk(x, mod, rows, s_dn, s_rw, s_lru, prm):
    depth = mod.shape[0]
    bsz, length, dm = x.shape
    fin_dn, fin_rw, fin_lru = [], [], []
    x = x.reshape(bsz * length, dm)
    h = norm_mod(x, prm["norm_g"][0, 0], mod[0], 0)
    for l in range(depth):
        g = prm["norm_g"][l]
        h = h.reshape(bsz, length, dm)
        i = l // 2
        if l % 2 == 0:
            mix, sd, sr = _even_mixer(
                h, prm["ev_w_in"][i], prm["dn_conv"][i], prm["dn_a_log"][i],
                prm["dn_dt_bias"][i], prm["dn_norm_g"][i], prm["rw_mu"][i], prm["rw_w0"][i],
                prm["rw_w2"][i], prm["rw_a0"][i], prm["rw_a2"][i], prm["rw_g2"][i], prm["rw_k_k"][i],
                prm["rw_k_a"][i], prm["rw_r_k"][i], prm["rw_ln_g"][i], prm["rw_ln_b"][i],
                s_dn[:, i], s_rw[:, i])
            fin_dn.append(sd)
            fin_rw.append(sr)
            w_out = prm["ev_w_out"][i]
        else:
            mix, sl = _odd_mixer(h, prm["od_w_in"][i], prm["lru_conv"][i],
                                 prm["lru_conv_b"][i], prm["lru_w_gate"][i], prm["lru_b_gate"][i],
                                 prm["lru_lambda"][i], s_lru[:, i])
            fin_lru.append(sl)
            w_out = prm["od_w_out"][i]
        x, h = matmul_residual(mix, w_out.astype(BF16), x, g[1], mod[l], 2, next_norm=(g[2], mod[l], 3),
                               tail_chunk=256)
        mid = ffn_up_conv(h, prm["ffn_w_up"][l].astype(BF16), prm["ffn_conv"][l].reshape(9, -1),
                          length // rows, rows)
        nxt = (prm["norm_g"][l + 1, 0], mod[l + 1], 0) if l + 1 < depth else None
        x, h = matmul_residual(mid, prm["ffn_w_down"][l].astype(BF16), x, g[3], mod[l], 5, next_norm=nxt,
                               tm=1024, tk=512, vmem=VMEM_LIMIT_LARGE_BYTES, tail_chunk=256)
    return (x.reshape(bsz, length, dm), jnp.stack(fin_dn, axis=1), jnp.stack(fin_rw, axis=1),
            jnp.stack(fin_lru, axis=1))


def kernel(x_prompt, x_sample, state_dn, state_rwkv, state_lru, c, c_ctx, w_mod, b_mod, norm_g, ffn_w_up, ffn_conv, ffn_w_down, ev_w_in, ev_w_out, dn_conv, dn_a_log, dn_dt_bias, dn_norm_g, rw_mu, rw_w0, rw_w2, rw_a0, rw_a2, rw_g2, rw_k_k, rw_k_a, rw_r_k, rw_ln_g, rw_ln_b, od_w_in, od_w_out, lru_conv, lru_conv_b, lru_w_gate, lru_b_gate, lru_lambda):
    prm = dict(w_mod=w_mod, b_mod=b_mod, norm_g=norm_g, ffn_w_up=ffn_w_up, ffn_conv=ffn_conv,
               ffn_w_down=ffn_w_down, ev_w_in=ev_w_in, ev_w_out=ev_w_out, dn_conv=dn_conv,
               dn_a_log=dn_a_log, dn_dt_bias=dn_dt_bias, dn_norm_g=dn_norm_g, rw_mu=rw_mu, rw_w0=rw_w0,
               rw_w2=rw_w2, rw_a0=rw_a0, rw_a2=rw_a2, rw_g2=rw_g2, rw_k_k=rw_k_k, rw_k_a=rw_k_a,
               rw_r_k=rw_r_k, rw_ln_g=rw_ln_g, rw_ln_b=rw_ln_b, od_w_in=od_w_in, od_w_out=od_w_out,
               lru_conv=lru_conv, lru_conv_b=lru_conv_b, lru_w_gate=lru_w_gate, lru_b_gate=lru_b_gate,
               lru_lambda=lru_lambda)
    bp = x_prompt.shape[0]
    grid_w = 64
    depth, dm = w_mod.shape[0], w_mod.shape[1]
    cond = jnp.concatenate([c_ctx[None, :], c], axis=0)
    mod = modulation_all(jnp.pad(cond, ((0, (-cond.shape[0]) % 16), (0, 0))), w_mod, b_mod)
    mod = mod.reshape(depth, -1, 6, 1, dm)
    y_prompt, new_dn, new_rw, new_lru = _trunk(
        x_prompt, mod[:, :1], 1,
        jnp.zeros((bp,) + state_dn.shape[1:], F32),
        jnp.zeros((bp,) + state_rwkv.shape[1:], F32),
        jnp.zeros((bp,) + state_lru.shape[1:], F32), prm)
    rows = x_sample.shape[1] // grid_w
    y_sample, _, _, _ = _trunk(x_sample, mod[:, 1:1 + c.shape[0]], rows, state_dn, state_rwkv, state_lru,
                               prm)
    return (y_prompt, y_sample, new_dn, new_rw, new_lru)
```
